```python
import functools
import jax, jax.numpy as jnp
from jax import lax
import numpy as np

D_MODEL = 1024
BATCH = 4
SEQ = 8192
DEPTH = 1
DEC_BATCH = 128
DEC_SEQ = 4
PAST_LEN = 16384
PAGE_SIZE = 128

N_HEADS = 16
N_KV_HEADS = 4
HEAD_DIM = D_MODEL // N_HEADS
GQA_GROUP = N_HEADS // N_KV_HEADS
WINDOW = 128
ATTN_W = N_HEADS * HEAD_DIM
KV_W = N_KV_HEADS * HEAD_DIM
D_CONV = D_MODEL
CONV_W = 3
IN_SIZES = (ATTN_W, KV_W, KV_W, D_CONV, D_CONV, D_CONV, ATTN_W, D_CONV)
IN_SPLITS = (ATTN_W, ATTN_W + KV_W, ATTN_W + 2 * KV_W, ATTN_W + 2 * KV_W + D_CONV,
             ATTN_W + 2 * KV_W + 2 * D_CONV, ATTN_W + 2 * KV_W + 3 * D_CONV,
             2 * ATTN_W + 2 * KV_W + 3 * D_CONV)
IN_W = 2 * ATTN_W + 2 * KV_W + 4 * D_CONV
PEER_HEADS = 8
PEER_KEYS = 128
PEER_EXPERTS = PEER_KEYS * PEER_KEYS
PEER_TOPK = 16
PEER_DKEY = 128
PEER_HALF = PEER_DKEY // 2
PEER_BLOCK = 256
RMS_EPS = 1e-6

kernel_name = "hybrid_swa_shortconv_peer_adaln_step"


def rms_norm(x, w):
    xf = x.astype(jnp.float32)
    y = xf * lax.rsqrt(jnp.mean(xf * xf, axis=-1, keepdims=True) + RMS_EPS)
    return (y * w.astype(jnp.float32)).astype(x.dtype)


def adaln(c, w_ada, b_ada):
    mod = jax.nn.silu(c) @ w_ada + b_ada
    return jnp.split(mod[:, None, :], 6, axis=-1)


def sink_attention(q, k, v, mask, sinks):
    s = jnp.einsum('...qhgd,...khd->...hgqk', q, k).astype(jnp.float32) * (HEAD_DIM ** -0.5)
    s = jnp.where(mask[..., None, None, :, :], s, -jnp.inf)
    sink = sinks.astype(jnp.float32).reshape(N_KV_HEADS, GQA_GROUP, 1, 1)
    m = jnp.maximum(jnp.max(s, axis=-1, keepdims=True), sink)
    p = jnp.exp(s - m)
    denom = jnp.sum(p, axis=-1, keepdims=True) + jnp.exp(sink - m)
    o = jnp.einsum('...hgqk,...khd->...qhgd', p / denom, v.astype(jnp.float32))
    return o.astype(q.dtype)


def prompt_attend(q, k, v, sinks):
    b, s = q.shape[:2]
    nb = s // WINDOW
    qb = q.reshape(b, nb, WINDOW, N_KV_HEADS, GQA_GROUP, HEAD_DIM)
    kb = k.reshape(b, nb, WINDOW, N_KV_HEADS, HEAD_DIM)
    vb = v.reshape(b, nb, WINDOW, N_KV_HEADS, HEAD_DIM)
    pad = ((0, 0), (1, 0), (0, 0), (0, 0), (0, 0))
    kk = jnp.concatenate([jnp.pad(kb, pad)[:, :-1], kb], axis=2)
    vv = jnp.concatenate([jnp.pad(vb, pad)[:, :-1], vb], axis=2)
    qi = jnp.arange(WINDOW)[:, None] + WINDOW
    kj = jnp.arange(2 * WINDOW)[None, :]
    rel = qi - kj
    band = (rel >= 0) & (rel < WINDOW)
    valid = (jnp.arange(nb)[:, None, None] * WINDOW + kj[None] - WINDOW) >= 0
    mask = band[None] & valid
    o = sink_attention(qb, kk, vv, mask, sinks)
    return o.reshape(b, s, ATTN_W), k[:, -WINDOW:], v[:, -WINDOW:]


def sample_attend(q, k, v, sinks, cache_k, cache_v):
    db, L = q.shape[:2]
    kk = jnp.concatenate([cache_k.astype(k.dtype), k], axis=1)
    vv = jnp.concatenate([cache_v.astype(v.dtype), v], axis=1)
    rel = (jnp.arange(L)[:, None] + WINDOW) - jnp.arange(WINDOW + L)[None, :]
    mask = (rel >= 0) & (rel < WINDOW)
    o = sink_attention(q.reshape(db, L, N_KV_HEADS, GQA_GROUP, HEAD_DIM), kk, vv, mask, sinks)
    return o.reshape(db, L, ATTN_W), kk[:, -WINDOW:], vv[:, -WINDOW:]


def short_conv(u, prev, w, bias):
    L = u.shape[1]
    up = jnp.concatenate([prev.astype(u.dtype), u], axis=1)
    y = up[:, 0:L] * w[0] + up[:, 1:L + 1] * w[1] + up[:, 2:L + 2] * w[2] + bias
    return y, up[:, -(CONV_W - 1):]


def peer_ffn(h, w_query, sub_keys, expert_u, expert_v):
    shp = h.shape
    t = h.reshape(-1, D_MODEL)
    n = t.shape[0]
    t = jnp.pad(t, ((0, (-n) % PEER_BLOCK), (0, 0)))
    blocks = t.reshape(-1, PEER_BLOCK, D_MODEL)
    keys = sub_keys.astype(jnp.float32)

    def one_block(xb):
        q = (xb @ w_query).reshape(PEER_BLOCK, PEER_HEADS, 2, PEER_HALF).astype(jnp.float32)
        s = jnp.einsum('thcd,hcnd->thcn', q, keys)
        sv, si = lax.top_k(s, PEER_TOPK)
        cand = sv[:, :, 0, :, None] + sv[:, :, 1, None, :]
        cv, ci = lax.top_k(cand.reshape(PEER_BLOCK, PEER_HEADS, PEER_TOPK * PEER_TOPK), PEER_TOPK)
        i1 = jnp.take_along_axis(si[:, :, 0], ci // PEER_TOPK, axis=-1)
        i2 = jnp.take_along_axis(si[:, :, 1], ci % PEER_TOPK, axis=-1)
        idx = i1 * PEER_KEYS + i2
        g = jax.nn.softmax(cv, axis=-1)
        a = jax.nn.gelu(jnp.einsum('thkd,td->thk', expert_u[idx], xb).astype(jnp.float32), approximate=False)
        wgt = (g * a).astype(xb.dtype)
        return jnp.einsum('thk,thkd->td', wgt, expert_v[idx])

    out = lax.map(one_block, blocks)
    return out.reshape(-1, D_MODEL)[:n].reshape(shp)


def block(x, c, conv_prev, attend, w_ada, b_ada, norm1_w, w_in, q_norm_w, k_norm_w, sinks,
          conv_w, conv_bias, w_out, norm2_w, w_query, sub_keys, expert_u, expert_v):
    b, L, _ = x.shape
    sh1, sc1, g1, sh2, sc2, g2 = adaln(c, w_ada, b_ada)
    h = rms_norm(x, norm1_w) * (1 + sc1) + sh1
    z = h @ w_in
    q, k, v, u, gate_post, gate_pre, g_attn, g_conv = jnp.split(z, IN_SPLITS, axis=-1)
    q = rms_norm(q.reshape(b, L, N_HEADS, HEAD_DIM), q_norm_w)
    k = rms_norm(k.reshape(b, L, N_KV_HEADS, HEAD_DIM), k_norm_w)
    v = v.reshape(b, L, N_KV_HEADS, HEAD_DIM)
    o_attn, new_k, new_v = attend(q, k, v, sinks)
    conv_y, new_conv = short_conv(gate_pre * u, conv_prev, conv_w, conv_bias)
    o_conv = gate_post * conv_y
    merged = jax.nn.sigmoid(g_attn) * o_attn + jax.nn.sigmoid(g_conv) * o_conv
    x = x + g1 * (merged @ w_out)
    h2 = rms_norm(x, norm2_w) * (1 + sc2) + sh2
    x = x + g2 * peer_ffn(h2, w_query, sub_keys, expert_u, expert_v)
    return x, new_k, new_v, new_conv


def setup_inputs(seed: int = 0) -> dict:
    key = jax.random.key(seed)
    ks = jax.random.split(key, 24)
    nrm = lambda k, shape, scale: jax.random.normal(k, shape, jnp.float32) * scale
    return {
        'x_prompt': nrm(ks[0], (BATCH, SEQ, D_MODEL), 1.0),
        'x_sample': nrm(ks[1], (DEC_BATCH, DEC_SEQ, D_MODEL), 1.0),
        'cache_k': nrm(ks[2], (DEPTH, DEC_BATCH, WINDOW, N_KV_HEADS, HEAD_DIM), 1.0),
        'cache_v': nrm(ks[3], (DEPTH, DEC_BATCH, WINDOW, N_KV_HEADS, HEAD_DIM), 1.0),
        'state_conv': nrm(ks[4], (DEPTH, DEC_BATCH, CONV_W - 1, D_CONV), 1.0),
        'c_prompt': nrm(ks[5], (BATCH, D_MODEL), 1.0),
        'c_sample': nrm(ks[6], (DEC_BATCH, D_MODEL), 1.0),
        'w_ada': nrm(ks[7], (DEPTH, D_MODEL, 6 * D_MODEL), D_MODEL ** -0.5),
        'b_ada': nrm(ks[8], (DEPTH, 6 * D_MODEL), 0.02),
        'norm1_w': 1.0 + nrm(ks[9], (DEPTH, D_MODEL), 0.02),
        'w_in': nrm(ks[10], (DEPTH, D_MODEL, IN_W), D_MODEL ** -0.5),
        'q_norm_w': 1.0 + nrm(ks[11], (DEPTH, HEAD_DIM), 0.02),
        'k_norm_w': 1.0 + nrm(ks[12], (DEPTH, HEAD_DIM), 0.02),
        'sinks': nrm(ks[13], (DEPTH, N_HEADS), 1.0),
        'conv_w': nrm(ks[14], (DEPTH, CONV_W, D_CONV), CONV_W ** -0.5),
        'conv_bias': nrm(ks[15], (DEPTH, D_CONV), 0.02),
        'w_out': nrm(ks[16], (DEPTH, ATTN_W, D_MODEL), ATTN_W ** -0.5),
        'norm2_w': 1.0 + nrm(ks[17], (DEPTH, D_MODEL), 0.02),
        'w_query': nrm(ks[18], (DEPTH, D_MODEL, PEER_HEADS * PEER_DKEY), D_MODEL ** -0.5),
        'sub_keys': nrm(ks[19], (DEPTH, PEER_HEADS, 2, PEER_KEYS, PEER_HALF), PEER_HALF ** -0.5),
        'expert_u': nrm(ks[20], (DEPTH, PEER_EXPERTS, D_MODEL), D_MODEL ** -0.5),
        'expert_v': nrm(ks[21], (DEPTH, PEER_EXPERTS, D_MODEL), 1.0),
    }


def reference(x_prompt, x_sample, cache_k, cache_v, state_conv, c_prompt, c_sample,
              w_ada, b_ada, norm1_w, w_in, q_norm_w, k_norm_w, sinks, conv_w, conv_bias,
              w_out, norm2_w, w_query, sub_keys, expert_u, expert_v):
    yp, ys = x_prompt, x_sample
    kp, vp, cp, ksm, vsm, csm = [], [], [], [], [], []
    for l in range(DEPTH):
        lw = (w_ada[l], b_ada[l], norm1_w[l], w_in[l], q_norm_w[l], k_norm_w[l], sinks[l],
              conv_w[l], conv_bias[l], w_out[l], norm2_w[l], w_query[l], sub_keys[l],
              expert_u[l], expert_v[l])
        conv0 = jnp.zeros((yp.shape[0], CONV_W - 1, D_CONV), yp.dtype)
        yp, k1, v1, c1 = block(yp, c_prompt, conv0, prompt_attend, *lw)
        attend_s = functools.partial(sample_attend, cache_k=cache_k[l], cache_v=cache_v[l])
        ys, k2, v2, c2 = block(ys, c_sample, state_conv[l], attend_s, *lw)
        kp.append(k1); vp.append(v1); cp.append(c1)
        ksm.append(k2); vsm.append(v2); csm.append(c2)
    return (yp, ys, jnp.stack(kp), jnp.stack(vp), jnp.stack(cp), jnp.stack(ksm), jnp.stack(vsm), jnp.stack(csm))
```

```python
import functools

import jax
import jax.numpy as jnp
from jax import lax
from jax.experimental import pallas as pl
from jax.experimental.pallas import tpu as pltpu

F32 = jnp.float32
BF16 = jnp.bfloat16

D_MODEL = 1024
N_HEADS = 16
N_KV_HEADS = 4
HEAD_DIM = 64
WINDOW = 128
KV_W = N_KV_HEADS * HEAD_DIM
CONV_W = 3
PEER_HEADS = 8
PEER_KEYS = 128
PEER_TOPK = 16
PEER_HALF = 64
PEER_EXPERTS = PEER_KEYS * PEER_KEYS
RMS_EPS = 1e-6

LANES = 128
VMEM_LIMIT = 56 * 1024 * 1024

PROJ_ROWS = 512
ATTN_ROWS = 512
ROUTE_ROWS = 512
PEER_ROWS = 512
PEER_CHUNK = 1024
SAMPLE_Q_ROWS = 8
SAMPLE_SEQ_BLOCK = 8

C_Q = 0
C_K = C_Q + D_MODEL
C_V = C_K + 2 * KV_W
C_U = C_V + 2 * KV_W
C_GPOST = C_U + D_MODEL
C_GPRE = C_GPOST + D_MODEL
C_GATTN = C_GPRE + D_MODEL
C_GCONV = C_GATTN + D_MODEL
C_END = C_GCONV + D_MODEL


def _params(*semantics):
    return pltpu.CompilerParams(dimension_semantics=semantics, vmem_limit_bytes=VMEM_LIMIT)


def _const_spec(shape):
    nd = len(shape)
    return pl.BlockSpec(shape, lambda *_: (0,) * nd, pipeline_mode=pl.Buffered(1))


def _mod_spec(arr, rows):
    if arr.shape[1] == 1:
        return pl.BlockSpec((None, 1, D_MODEL), lambda b, i, *_: (b, 0, 0))
    return pl.BlockSpec((None, rows, D_MODEL), lambda b, i, *_: (b, i, 0))


def _dot(a, b):
    return jnp.dot(a, b, preferred_element_type=F32)


def _dot_nt(a, b):
    return lax.dot_general(a, b, (((1,), (1,)), ((), ())), preferred_element_type=F32)


def _dot_split(a, b):
    hi = a.astype(BF16)
    lo = (a - hi.astype(F32)).astype(BF16)
    return _dot(hi, b) + _dot(lo, b)


def _ada_kernel(c_ref, w_ref, b_ref, o_ref):
    s = jax.nn.silu(c_ref[...])
    o_ref[...] = jnp.dot(s, w_ref[...], precision=lax.Precision.HIGHEST,
                         preferred_element_type=F32) + b_ref[...]


def _adaln(c_all, w_ada, b_ada):
    rows = c_all.shape[0]
    return pl.pallas_call(
        _ada_kernel,
        out_shape=jax.ShapeDtypeStruct((6, rows, D_MODEL), F32),
        grid=(6,),
        in_specs=[
            pl.BlockSpec((rows, D_MODEL), lambda j: (0, 0)),
            pl.BlockSpec((D_MODEL, D_MODEL), lambda j: (0, j)),
            pl.BlockSpec((1, D_MODEL), lambda j: (0, j)),
        ],
        out_specs=pl.BlockSpec((None, rows, D_MODEL), lambda j: (j, 0, 0)),
        compiler_params=_params("arbitrary"),
        name="adaln",
    )(c_all, w_ada, b_ada.reshape(1, 6 * D_MODEL))


def _head_rms(z, hsum, hexp, w):
    ss = _dot_split(z * z, hsum)
    r = lax.rsqrt(ss * (1.0 / HEAD_DIM) + RMS_EPS)
    return z * _dot_split(r, hexp) * w


def _proj_kernel(x_ref, sc_ref, sh_ref, nw_ref, win_ref, qw_ref, kw_ref, hsum_ref, hexp_ref,
                 q_ref, k2_ref, v2_ref, cu_ref, gc_ref, sa_ref):
    x = x_ref[...]
    h = x * lax.rsqrt(jnp.mean(x * x, axis=-1, keepdims=True) + RMS_EPS) * nw_ref[...]
    h = h * (1.0 + sc_ref[...]) + sh_ref[...]
    hb = h.astype(BF16)

    def seg(lo, hi):
        return _dot(hb, win_ref[:, lo:hi])

    q = _head_rms(seg(C_Q, C_K), hsum_ref[...], hexp_ref[...], qw_ref[...])
    q_ref[...] = (q * (HEAD_DIM ** -0.5)).astype(BF16)
    nk = 2 * KV_W
    k2_ref[...] = _head_rms(seg(C_K, C_V), hsum_ref[:nk, :], hexp_ref[:, :nk], kw_ref[...])
    v2_ref[...] = seg(C_V, C_U)
    cu_ref[...] = seg(C_GPRE, C_GATTN) * seg(C_U, C_GPOST)
    gc_ref[...] = (seg(C_GPOST, C_GPRE) * jax.nn.sigmoid(seg(C_GCONV, C_END))).astype(BF16)
    sa_ref[...] = jax.nn.sigmoid(seg(C_GATTN, C_GCONV)).astype(BF16)


def _project(x, sc1, sh1, nw, win2, qw, kw, hsum, hexp):
    nb, s, _ = x.shape
    rows = min(PROJ_ROWS, s)
    tok = lambda w, dt: jax.ShapeDtypeStruct((nb, s, w), dt)
    tspec = lambda w: pl.BlockSpec((None, rows, w), lambda b, i: (b, i, 0))
    return pl.pallas_call(
        _proj_kernel,
        out_shape=(tok(D_MODEL, BF16), tok(2 * KV_W, F32), tok(2 * KV_W, F32),
                   tok(D_MODEL, F32), tok(D_MODEL, BF16), tok(D_MODEL, BF16)),
        grid=(nb, s // rows),
        in_specs=[
            tspec(D_MODEL), _mod_spec(sc1, rows), _mod_spec(sh1, rows),
            _const_spec((1, D_MODEL)), _const_spec((D_MODEL, C_END)),
            _const_spec((1, D_MODEL)), _const_spec((1, 2 * KV_W)),
            _const_spec((D_MODEL, LANES)), _const_spec((LANES, D_MODEL)),
        ],
        out_specs=(tspec(D_MODEL), tspec(2 * KV_W), tspec(2 * KV_W),
                   tspec(D_MODEL), tspec(D_MODEL), tspec(D_MODEL)),
        compiler_params=_params("parallel", "parallel"),
        name="in_proj",
    )(x, sc1, sh1, nw, win2, qw, kw, hsum, hexp)


def _attend(qblk, k2, v2, sinks_ref, first_col):
    qb = qblk.shape[0]
    lane = lax.broadcasted_iota(jnp.int32, (qb, LANES), 1)
    low = lane < HEAD_DIM
    row = lax.broadcasted_iota(jnp.int32, (4 * qb, 2 * WINDOW), 0)
    col = lax.broadcasted_iota(jnp.int32, (4 * qb, 2 * WINDOW), 1)
    dist = col - (row & (qb - 1))
    mask = (dist >= 1) & (dist <= WINDOW) & (col >= first_col)
    blk = lax.broadcasted_iota(jnp.int32, (4 * qb, 1), 0) // qb
    ones = jnp.ones((2 * WINDOW, LANES), BF16)
    outs = []
    for j in range(N_KV_HEADS):
        kj = k2[:, j * LANES:(j + 1) * LANES]
        vj = v2[:, j * LANES:(j + 1) * LANES]
        qa = qblk[:, 2 * j * LANES:(2 * j + 1) * LANES].astype(F32)
        qc = qblk[:, (2 * j + 1) * LANES:(2 * j + 2) * LANES].astype(F32)
        lhs = jnp.concatenate([jnp.where(low, qa, 0.0), jnp.where(low, 0.0, qa),
                               jnp.where(low, qc, 0.0), jnp.where(low, 0.0, qc)], axis=0)
        s = jnp.where(mask, _dot_nt(lhs.astype(BF16), kj), -jnp.inf)
        sink = jnp.full((4 * qb, 1), sinks_ref[4 * j + 3], F32)
        for g in range(2, -1, -1):
            sink = jnp.where(blk == g, sinks_ref[4 * j + g], sink)
        m = jnp.maximum(jnp.max(s, axis=-1, keepdims=True), sink)
        p = jnp.exp(s - m).astype(BF16)
        o = _dot(p, jnp.concatenate([vj, ones], axis=1))
        on = o[:, :LANES] / (o[:, LANES:] + jnp.exp(sink - m))
        outs.append(jnp.where(low, on[0:qb], on[qb:2 * qb]))
        outs.append(jnp.where(low, on[2 * qb:3 * qb], on[3 * qb:4 * qb]))
    return jnp.concatenate(outs, axis=1)


def _merge_tail(x, cu, prev1, prev2, gc, sa, oattn, g1, sc2, sh2, cw_ref, cb, wout, n2w):
    conv_y = prev2 * cw_ref[0:1, :] + prev1 * cw_ref[1:2, :] + cu * cw_ref[2:3, :] + cb
    merged = sa.astype(F32) * oattn + gc.astype(F32) * conv_y
    x1 = x + g1 * _dot(merged.astype(BF16), wout)
    h2 = x1 * lax.rsqrt(jnp.mean(x1 * x1, axis=-1, keepdims=True) + RMS_EPS) * n2w
    h2 = h2 * (1.0 + sc2) + sh2
    return x1, h2.astype(BF16)


def _prompt_mix_kernel(sinks_ref, x_ref, q_ref, k_ref, kp_ref, v_ref, vp_ref, cu_ref, cup_ref,
                       gc_ref, sa_ref, g1_ref, sc2_ref, sh2_ref, cw_ref, cb_ref, wout_ref, n2w_ref,
                       x1_ref, h2_ref, oat_scr):
    i = pl.program_id(1)
    rows = x_ref.shape[0]
    kext = jnp.concatenate([kp_ref[...], k_ref[...]], axis=0).astype(BF16)
    vext = jnp.concatenate([vp_ref[...], v_ref[...]], axis=0).astype(BF16)
    for w in range(rows // WINDOW):
        first_col = jnp.where(i == 0, WINDOW, 0) if w == 0 else 0
        lo = w * WINDOW
        oat_scr[lo:lo + WINDOW, :] = _attend(
            q_ref[lo:lo + WINDOW, :], kext[lo:lo + 2 * WINDOW, :], vext[lo:lo + 2 * WINDOW, :],
            sinks_ref, first_col)

    cu = cu_ref[...]
    has_prev = (i > 0).astype(F32)
    p1 = cup_ref[7:8, :] * has_prev
    p2 = cup_ref[6:7, :] * has_prev
    r = lax.broadcasted_iota(jnp.int32, (rows, 1), 0)
    prev1 = jnp.where(r == 0, p1, pltpu.roll(cu, 1, axis=0))
    prev2 = jnp.where(r == 0, p2, jnp.where(r == 1, p1, pltpu.roll(cu, 2, axis=0)))
    x1, h2 = _merge_tail(x_ref[...], cu, prev1, prev2, gc_ref[...], sa_ref[...], oat_scr[...],
                         g1_ref[...], sc2_ref[...], sh2_ref[...], cw_ref, cb_ref[...],
                         wout_ref[...], n2w_ref[...])
    x1_ref[...] = x1
    h2_ref[...] = h2


def _prompt_mix(sinks, x, q, k2, v2, cu, gc, sa, g1, sc2, sh2, cw, cb, wout, n2w):
    nb, s, _ = x.shape
    rows = ATTN_ROWS
    wpt = rows // WINDOW
    tspec = lambda w: pl.BlockSpec((None, rows, w), lambda b, i: (b, i, 0))
    prev_win = pl.BlockSpec((None, WINDOW, 2 * KV_W), lambda b, i: (b, jnp.maximum(i * wpt - 1, 0), 0))
    prev_rows = pl.BlockSpec((None, 8, D_MODEL), lambda b, i: (b, jnp.maximum(i * (rows // 8) - 1, 0), 0))
    mod = pl.BlockSpec((None, 1, D_MODEL), lambda b, i: (b, 0, 0))
    return pl.pallas_call(
        _prompt_mix_kernel,
        out_shape=(jax.ShapeDtypeStruct((nb, s, D_MODEL), F32),
                   jax.ShapeDtypeStruct((nb, s, D_MODEL), BF16)),
        grid=(nb, s // rows),
        in_specs=[
            pl.BlockSpec(memory_space=pltpu.SMEM),
            tspec(D_MODEL), tspec(D_MODEL),
            tspec(2 * KV_W), prev_win, tspec(2 * KV_W), prev_win,
            tspec(D_MODEL), prev_rows, tspec(D_MODEL), tspec(D_MODEL),
            mod, mod, mod,
            _const_spec((CONV_W, D_MODEL)), _const_spec((1, D_MODEL)),
            _const_spec((D_MODEL, D_MODEL)), _const_spec((1, D_MODEL)),
        ],
        out_specs=(tspec(D_MODEL), tspec(D_MODEL)),
        scratch_shapes=[pltpu.VMEM((rows, D_MODEL), F32)],
        compiler_params=_params("parallel", "parallel"),
        name="prompt_mix",
    )(sinks, x, q, k2, k2, v2, v2, cu, cu, gc, sa, g1, sc2, sh2, cw, cb, wout, n2w)


def _sample_attn_kernel(sinks_ref, q_ref, ck_ref, kn_ref, cv_ref, vn_ref, o_ref):
    pad = jnp.zeros((WINDOW - SAMPLE_Q_ROWS, 2 * KV_W), BF16)
    for n in range(q_ref.shape[0]):
        kext = jnp.concatenate([ck_ref[n], kn_ref[n], pad], axis=0)
        vext = jnp.concatenate([cv_ref[n], vn_ref[n], pad], axis=0)
        o_ref[n] = _attend(q_ref[n], kext, vext, sinks_ref, 0)


def _sample_attn(sinks, q8, ck2, k2n, cv2, v2n):
    nseq = q8.shape[0]
    sb = SAMPLE_SEQ_BLOCK
    spec = lambda r, w: pl.BlockSpec((sb, r, w), lambda n: (n, 0, 0))
    return pl.pallas_call(
        _sample_attn_kernel,
        out_shape=jax.ShapeDtypeStruct((nseq, SAMPLE_Q_ROWS, D_MODEL), F32),
        grid=(nseq // sb,),
        in_specs=[
            pl.BlockSpec(memory_space=pltpu.SMEM),
            spec(SAMPLE_Q_ROWS, D_MODEL),
            spec(WINDOW, 2 * KV_W), spec(SAMPLE_Q_ROWS, 2 * KV_W),
            spec(WINDOW, 2 * KV_W), spec(SAMPLE_Q_ROWS, 2 * KV_W),
        ],
        out_specs=spec(SAMPLE_Q_ROWS, D_MODEL),
        compiler_params=_params("parallel"),
        name="sample_attn",
    )(sinks, q8, ck2, k2n, cv2, v2n)


def _sample_mix_kernel(x_ref, cu_ref, p1_ref, p2_ref, gc_ref, sa_ref, oat_ref,
                       g1_ref, sc2_ref, sh2_ref, cw_ref, cb_ref, wout_ref, n2w_ref,
                       x1_ref, h2_ref):
    x1, h2 = _merge_tail(x_ref[...], cu_ref[...], p1_ref[...], p2_ref[...], gc_ref[...],
                         sa_ref[...], oat_ref[...], g1_ref[...], sc2_ref[...], sh2_ref[...],
                         cw_ref, cb_ref[...], wout_ref[...], n2w_ref[...])
    x1_ref[...] = x1
    h2_ref[...] = h2


def _sample_mix(x, cu, prev1, prev2, gc, sa, oattn, g1, sc2, sh2, cw, cb, wout, n2w):
    t = x.shape[0]
    full = lambda: pl.BlockSpec((t, D_MODEL), lambda: (0, 0))
    return pl.pallas_call(
        _sample_mix_kernel,
        out_shape=(jax.ShapeDtypeStruct((t, D_MODEL), F32),
                   jax.ShapeDtypeStruct((t, D_MODEL), BF16)),
        in_specs=[full() for _ in range(10)] + [
            pl.BlockSpec((CONV_W, D_MODEL), lambda: (0, 0)),
            pl.BlockSpec((1, D_MODEL), lambda: (0, 0)),
            pl.BlockSpec((D_MODEL, D_MODEL), lambda: (0, 0)),
            pl.BlockSpec((1, D_MODEL), lambda: (0, 0)),
        ],
        out_specs=(full(), full()),
        compiler_params=pltpu.CompilerParams(vmem_limit_bytes=VMEM_LIMIT),
        name="sample_mix",
    )(x, cu, prev1, prev2, gc, sa, oattn, g1, sc2, sh2, cw, cb, wout, n2w)


def _top16(s, index):
    rank = jnp.full(s.shape, float(PEER_TOPK), F32)
    vals = []
    for r in range(PEER_TOPK):
        m = jnp.max(s, axis=0, keepdims=True)
        first = jnp.min(jnp.where(s == m, index, float(s.shape[0])), axis=0, keepdims=True)
        hit = index == first
        rank = jnp.where(hit, float(r), rank)
        s = jnp.where(hit, -jnp.inf, s)
        vals.append(m)
    return vals, rank


def _route_kernel(h2_ref, wqt_ref, keys_ref, n1_ref, e1_ref, r2_ref, e2_ref, qpt_scr):
    rows = h2_ref.shape[0]
    qpt_scr[...] = _dot_nt(wqt_ref[...], h2_ref[...])
    idx16 = lax.broadcasted_iota(jnp.int32, (PEER_TOPK, LANES), 0).astype(F32)
    idx128 = lax.broadcasted_iota(jnp.int32, (PEER_KEYS, LANES), 0).astype(F32)
    idx256 = lax.broadcasted_iota(jnp.int32, (PEER_TOPK * PEER_TOPK, LANES), 0).astype(F32)

    def body(it, carry):
        tc = it // PEER_HEADS
        h = it % PEER_HEADS
        ls = pl.ds(pl.multiple_of(tc * LANES, LANES), LANES)
        base = pl.multiple_of(h * (2 * PEER_HALF), 2 * PEER_HALF)
        q1 = qpt_scr[pl.ds(base, PEER_HALF), ls]
        q2 = qpt_scr[pl.ds(base + PEER_HALF, PEER_HALF), ls]
        hp = lax.Precision.HIGHEST
        s1 = jnp.dot(keys_ref[2 * h], q1, precision=hp, preferred_element_type=F32)
        s2 = jnp.dot(keys_ref[2 * h + 1], q2, precision=hp, preferred_element_type=F32)
        v1, rank1 = _top16(s1, idx128)
        v2, rank2 = _top16(s2, idx128)
        v2a = jnp.zeros((PEER_TOPK, LANES), F32)
        for r in range(PEER_TOPK):
            v2a = jnp.where(idx16 == float(r), v2[r], v2a)
        cand = jnp.concatenate([v1[r] + v2a for r in range(PEER_TOPK)], axis=0)
        sel = jnp.zeros(cand.shape, F32)
        c = cand
        for _ in range(PEER_TOPK):
            m = jnp.max(c, axis=0, keepdims=True)
            first = jnp.min(jnp.where(c == m, idx256, float(PEER_TOPK * PEER_TOPK)),
                            axis=0, keepdims=True)
            hit = idx256 == first
            sel = jnp.where(hit, 1.0, sel)
            c = jnp.where(hit, -jnp.inf, c)
        cmax = v1[0] + v2a[0:1]
        z = jnp.sum(sel * jnp.exp(cand - cmax), axis=0, keepdims=True)
        n1 = jnp.zeros(rank1.shape, F32)
        for r in range(PEER_TOPK):
            n_r = jnp.sum(sel[r * PEER_TOPK:(r + 1) * PEER_TOPK], axis=0, keepdims=True)
            n1 = jnp.where(rank1 == float(r), n_r, n1)
        n1_ref[h, :, ls] = n1
        e1_ref[h, :, ls] = jnp.exp(s1 - v1[0]) / z
        r2_ref[h, :, ls] = rank2
        e2_ref[h, :, ls] = jnp.exp(s2 - v2[0])
        return carry

    lax.fori_loop(0, (rows // LANES) * PEER_HEADS, body, 0)


def _route(h2, wqt, keys):
    t = h2.shape[0]
    rows = min(ROUTE_ROWS, t)
    out = jax.ShapeDtypeStruct((PEER_HEADS, PEER_KEYS, t), F32)
    ospec = pl.BlockSpec((PEER_HEADS, PEER_KEYS, rows), lambda i: (0, 0, i))
    return pl.pallas_call(
        _route_kernel,
        out_shape=(out, out, out, out),
        grid=(t // rows,),
        in_specs=[
            pl.BlockSpec((rows, D_MODEL), lambda i: (i, 0)),
            _const_spec((D_MODEL, D_MODEL)),
            _const_spec((2 * PEER_HEADS, PEER_KEYS, PEER_HALF)),
        ],
        out_specs=(ospec, ospec, ospec, ospec),
        scratch_shapes=[pltpu.VMEM((D_MODEL, rows), F32)],
        compiler_params=_params("parallel"),
        name="peer_route",
    )(h2, wqt, keys)


def _peer_kernel(h2_ref, u_ref, vt_ref, n1_ref, e1_ref, r2_ref, e2_ref, x1_ref, g2_ref,
                 y_ref, at_scr, hw_scr, acc_scr):
    c = pl.program_id(2)
    rows = h2_ref.shape[0]

    @pl.when(c == 0)
    def _():
        acc_scr[...] = jnp.zeros_like(acc_scr)

    at_scr[...] = _dot_nt(u_ref[...], h2_ref[...])

    def body(tc, carry):
        ls = pl.ds(pl.multiple_of(tc * LANES, LANES), LANES)
        for il in range(PEER_CHUNK // PEER_KEYS):
            gate = jnp.zeros((PEER_KEYS, LANES), F32)
            for h in range(PEER_HEADS):
                keep = r2_ref[h, :, ls] < n1_ref[h, il:il + 1, ls]
                gate = gate + jnp.where(keep, e2_ref[h, :, ls], 0.0) * e1_ref[h, il:il + 1, ls]
            es = slice(il * PEER_KEYS, (il + 1) * PEER_KEYS)
            a = at_scr[es, ls]
            act = 0.5 * a * (1.0 + lax.erf(a * (2.0 ** -0.5)))
            hw_scr[es, ls] = (gate * act).astype(BF16)
        return carry

    lax.fori_loop(0, rows // LANES, body, 0)
    acc_scr[...] += _dot(vt_ref[...], hw_scr[...])

    @pl.when(c == pl.num_programs(2) - 1)
    def _():
        y_ref[...] = x1_ref[...] + g2_ref[...] * acc_scr[...].T


def _peer(h2, u, vt, n1, e1, r2, e2, x1, g2):
    nb, s, _ = x1.shape
    rows = min(PEER_ROWS, s)
    spb = s // rows
    il_rows = PEER_CHUNK // PEER_KEYS
    tspec = lambda: pl.BlockSpec((None, rows, D_MODEL), lambda b, i, c: (b, i, 0))
    per_chunk = lambda: pl.BlockSpec((PEER_HEADS, il_rows, rows), lambda b, i, c: (0, c, b * spb + i))
    per_block = lambda: pl.BlockSpec((PEER_HEADS, PEER_KEYS, rows), lambda b, i, c: (0, 0, b * spb + i))
    return pl.pallas_call(
        _peer_kernel,
        out_shape=jax.ShapeDtypeStruct((nb, s, D_MODEL), F32),
        grid=(nb, spb, PEER_EXPERTS // PEER_CHUNK),
        in_specs=[
            tspec(),
            pl.BlockSpec((PEER_CHUNK, D_MODEL), lambda b, i, c: (c, 0)),
            pl.BlockSpec((D_MODEL, PEER_CHUNK), lambda b, i, c: (0, c)),
            per_chunk(), per_chunk(), per_block(), per_block(),
            tspec(), _mod_spec(g2, rows),
        ],
        out_specs=tspec(),
        scratch_shapes=[pltpu.VMEM((PEER_CHUNK, rows), F32),
                        pltpu.VMEM((PEER_CHUNK, rows), BF16),
                        pltpu.VMEM((D_MODEL, rows), F32)],
        compiler_params=_params("parallel", "parallel", "arbitrary"),
        name="peer_experts",
    )(h2, u, vt, n1, e1, r2, e2, x1, g2)


def _dup_heads(w):
    lead = w.shape[:-1]
    w = w.reshape(lead + (N_KV_HEADS, 1, HEAD_DIM))
    return jnp.broadcast_to(w, lead + (N_KV_HEADS, 2, HEAD_DIM)).reshape(lead + (2 * KV_W,))


def _first_of_dup(a):
    return a.reshape(a.shape[:-1] + (N_KV_HEADS, 2, HEAD_DIM))[..., 0, :]


def kernel(x_prompt, x_sample, cache_k, cache_v, state_conv, c_prompt, c_sample, w_ada, b_ada, norm1_w, w_in, q_norm_w, k_norm_w, sinks, conv_w, conv_bias, w_out, norm2_w, w_query, sub_keys, expert_u, expert_v):
    depth = w_ada.shape[0]
    batch, seq, _ = x_prompt.shape
    nseq, dec = x_sample.shape[:2]
    assert dec <= SAMPLE_Q_ROWS and dec >= CONV_W - 1
    ts = nseq * dec

    head_of_col = jnp.arange(D_MODEL, dtype=jnp.int32) // HEAD_DIM
    hsum = (head_of_col[:, None] == jnp.arange(LANES, dtype=jnp.int32)[None, :]).astype(BF16)
    hexp = hsum.T
    c_all = jnp.concatenate([c_prompt, c_sample], axis=0)
    c_all = jnp.pad(c_all, ((0, (-c_all.shape[0]) % 8), (0, 0)))

    yp = x_prompt
    ys = x_sample.reshape(1, ts, D_MODEL)
    kp, vp, cp, ksm, vsm, csm = [], [], [], [], [], []
    for l in range(depth):
        win = w_in[l]
        win2 = jnp.concatenate(
            [win[:, :D_MODEL], _dup_heads(win[:, D_MODEL:D_MODEL + KV_W]),
             _dup_heads(win[:, D_MODEL + KV_W:D_MODEL + 2 * KV_W]), win[:, D_MODEL + 2 * KV_W:]],
            axis=1).astype(BF16)
        nw1 = norm1_w[l].reshape(1, D_MODEL)
        nw2 = norm2_w[l].reshape(1, D_MODEL)
        qw = jnp.tile(q_norm_w[l], N_HEADS).reshape(1, D_MODEL)
        kw = jnp.tile(k_norm_w[l], 2 * N_KV_HEADS).reshape(1, 2 * KV_W)
        cw = conv_w[l]
        cb = conv_bias[l].reshape(1, D_MODEL)
        wout = w_out[l].astype(BF16)
        wqt = w_query[l].T.astype(BF16)
        keys = sub_keys[l].reshape(2 * PEER_HEADS, PEER_KEYS, PEER_HALF)
        u = expert_u[l].astype(BF16)
        vt = expert_v[l].T.astype(BF16)

        mod = _adaln(c_all, w_ada[l], b_ada[l])
        mod_p = [mod[j, :batch].reshape(batch, 1, D_MODEL) for j in range(6)]
        mod_s = [jnp.repeat(mod[j, batch:batch + nseq], dec, axis=0).reshape(1, ts, D_MODEL)
                 for j in range(6)]

        q, k2, v2, cu, gc, sa = _project(yp, mod_p[1], mod_p[0], nw1, win2, qw, kw, hsum, hexp)
        x1, h2 = _prompt_mix(sinks[l], yp, q, k2, v2, cu, gc, sa, mod_p[2], mod_p[4], mod_p[3],
                             cw, cb, wout, nw2)
        n1, e1, r2, e2 = _route(h2.reshape(batch * seq, D_MODEL), wqt, keys)
        yp = _peer(h2, u, vt, n1, e1, r2, e2, x1, mod_p[5])
        kp.append(_first_of_dup(k2[:, seq - WINDOW:]))
        vp.append(_first_of_dup(v2[:, seq - WINDOW:]))
        cp.append(cu[:, seq - (CONV_W - 1):])

        q, k2, v2, cu, gc, sa = _project(ys, mod_s[1], mod_s[0], nw1, win2, qw, kw, hsum, hexp)
        pad_rows = lambda a: jnp.pad(a.reshape(nseq, dec, a.shape[-1]),
                                     ((0, 0), (0, SAMPLE_Q_ROWS - dec), (0, 0)))
        oat = _sample_attn(sinks[l], pad_rows(q[0]),
                           _dup_heads(cache_k[l].reshape(nseq, WINDOW, KV_W)).astype(BF16),
                           pad_rows(k2[0]).astype(BF16),
                           _dup_heads(cache_v[l].reshape(nseq, WINDOW, KV_W)).astype(BF16),
                           pad_rows(v2[0]).astype(BF16))
        oat = oat[:, :dec].reshape(ts, D_MODEL)
        cu_seq = cu.reshape(nseq, dec, D_MODEL)
        up = jnp.concatenate([state_conv[l], cu_seq], axis=1)
        prev2 = up[:, 0:dec].reshape(ts, D_MODEL)
        prev1 = up[:, 1:dec + 1].reshape(ts, D_MODEL)
        flat = lambda a: a.reshape(ts, D_MODEL)
        x1, h2 = _sample_mix(flat(ys), flat(cu), prev1, prev2, flat(gc), flat(sa), oat,
                             flat(mod_s[2]), flat(mod_s[4]), flat(mod_s[3]), cw, cb, wout, nw2)
        n1, e1, r2, e2 = _route(h2, wqt, keys)
        ys = _peer(h2.reshape(1, ts, D_MODEL), u, vt, n1, e1, r2, e2,
                   x1.reshape(1, ts, D_MODEL), mod_s[5])
        k_new = _first_of_dup(k2[0]).reshape(nseq, dec, N_KV_HEADS, HEAD_DIM)
        v_new = _first_of_dup(v2[0]).reshape(nseq, dec, N_KV_HEADS, HEAD_DIM)
        ksm.append(jnp.concatenate([cache_k[l], k_new], axis=1)[:, -WINDOW:])
        vsm.append(jnp.concatenate([cache_v[l], v_new], axis=1)[:, -WINDOW:])
        csm.append(up[:, -(CONV_W - 1):])

    return (yp, ys.reshape(nseq, dec, D_MODEL), jnp.stack(kp), jnp.stack(vp), jnp.stack(cp),
            jnp.stack(ksm), jnp.stack(vsm), jnp.stack(csm))
```

```python
import functools

import jax
import jax.numpy as jnp
from jax import lax
from jax.experimental import pallas as pl
from jax.experimental.pallas import tpu as pltpu

F32 = jnp.float32
BF16 = jnp.bfloat16

D_MODEL = 1024
N_HEADS = 16
N_KV_HEADS = 4
HEAD_DIM = 64
WINDOW = 128
KV_W = N_KV_HEADS * HEAD_DIM
CONV_W = 3
PEER_HEADS = 8
PEER_KEYS = 128
PEER_TOPK = 16
PEER_HALF = 64
PEER_EXPERTS = PEER_KEYS * PEER_KEYS
RMS_EPS = 1e-6

LANES = 128
VMEM_LIMIT = 56 * 1024 * 1024

PROJ_ROWS = 512
ATTN_ROWS = 512
PACK_ROWS = 1024
ROUTE_ROWS = 512
ROUTE_HEADS = 2
PEER_ROWS = 512
PEER_CHUNK = 1024
SAMPLE_Q_ROWS = 8
SAMPLE_SEQ_BLOCK = 8

C_Q = 0
C_K = C_Q + D_MODEL
C_V = C_K + 2 * KV_W
C_U = C_V + 2 * KV_W
C_GPOST = C_U + D_MODEL
C_GPRE = C_GPOST + D_MODEL
C_GATTN = C_GPRE + D_MODEL
C_GCONV = C_GATTN + D_MODEL
C_END = C_GCONV + D_MODEL


def _params(*semantics):
    return pltpu.CompilerParams(dimension_semantics=semantics, vmem_limit_bytes=VMEM_LIMIT)


def _const_spec(shape):
    nd = len(shape)
    return pl.BlockSpec(shape, lambda *_: (0,) * nd, pipeline_mode=pl.Buffered(1))


def _mod_spec(arr, rows):
    if arr.shape[1] == 1:
        return pl.BlockSpec((None, 1, D_MODEL), lambda b, i, *_: (b, 0, 0))
    return pl.BlockSpec((None, rows, D_MODEL), lambda b, i, *_: (b, i, 0))


def _dot(a, b):
    return jnp.dot(a, b, preferred_element_type=F32)


def _dot_nt(a, b):
    return lax.dot_general(a, b, (((1,), (1,)), ((), ())), preferred_element_type=F32)


def _pack_words(x):
    return pltpu.bitcast(x.astype(BF16), jnp.uint32)


def _unpack_words(w):
    return pltpu.bitcast(w, BF16)


def _dot_split(a, b):
    hi = a.astype(BF16)
    lo = (a - hi.astype(F32)).astype(BF16)
    return _dot(hi, b) + _dot(lo, b)


def _ada_kernel(c_ref, w_ref, b_ref, o_ref):
    s = jax.nn.silu(c_ref[...])
    o_ref[...] = jnp.dot(s, w_ref[...], precision=lax.Precision.HIGHEST,
                         preferred_element_type=F32) + b_ref[...]


def _adaln(c_all, w_ada, b_ada):
    rows = c_all.shape[0]
    return pl.pallas_call(
        _ada_kernel,
        out_shape=jax.ShapeDtypeStruct((6, rows, D_MODEL), F32),
        grid=(6,),
        in_specs=[
            pl.BlockSpec((rows, D_MODEL), lambda j: (0, 0)),
            pl.BlockSpec((D_MODEL, D_MODEL), lambda j: (0, j)),
            pl.BlockSpec((1, D_MODEL), lambda j: (0, j)),
        ],
        out_specs=pl.BlockSpec((None, rows, D_MODEL), lambda j: (j, 0, 0)),
        compiler_params=_params("arbitrary"),
        name="adaln",
    )(c_all, w_ada, b_ada.reshape(1, 6 * D_MODEL))


def _pack_kernel(x_ref, o_ref):
    o_ref[...] = _pack_words(x_ref[...])


def _pack_t_kernel(x_ref, o_ref):
    o_ref[...] = _pack_words(x_ref[...].T)


def _pack_table(x):
    n, m = x.shape
    return pl.pallas_call(
        _pack_kernel,
        out_shape=jax.ShapeDtypeStruct((n // 2, m), jnp.uint32),
        grid=(n // PACK_ROWS,),
        in_specs=[pl.BlockSpec((PACK_ROWS, m), lambda i: (i, 0))],
        out_specs=pl.BlockSpec((PACK_ROWS // 2, m), lambda i: (i, 0)),
        compiler_params=_params("parallel"),
        name="pack_table",
    )(x)


def _pack_table_transposed(x):
    n, m = x.shape
    return pl.pallas_call(
        _pack_t_kernel,
        out_shape=jax.ShapeDtypeStruct((m // 2, n), jnp.uint32),
        grid=(n // PACK_ROWS,),
        in_specs=[pl.BlockSpec((PACK_ROWS, m), lambda i: (i, 0))],
        out_specs=pl.BlockSpec((m // 2, PACK_ROWS), lambda i: (0, i)),
        compiler_params=_params("parallel"),
        name="pack_table_t",
    )(x)


def _head_rms(z, hsum, hexp, w):
    ss = _dot_split(z * z, hsum)
    r = lax.rsqrt(ss * (1.0 / HEAD_DIM) + RMS_EPS)
    return z * _dot_split(r, hexp) * w


def _proj_kernel(x_ref, sc_ref, sh_ref, nw_ref, win_ref, qw_ref, kw_ref, hsum_ref, hexp_ref,
                 q_ref, k2_ref, v2_ref, cu_ref, gc_ref, sa_ref):
    x = x_ref[...]
    h = x * lax.rsqrt(jnp.mean(x * x, axis=-1, keepdims=True) + RMS_EPS) * nw_ref[...]
    h = h * (1.0 + sc_ref[...]) + sh_ref[...]
    hb = h.astype(BF16)

    def seg(lo, hi):
        return _dot(hb, win_ref[:, lo:hi])

    q = _head_rms(seg(C_Q, C_K), hsum_ref[...], hexp_ref[...], qw_ref[...])
    q_ref[...] = (q * (HEAD_DIM ** -0.5)).astype(BF16)
    nk = 2 * KV_W
    k2_ref[...] = _head_rms(seg(C_K, C_V), hsum_ref[:nk, :], hexp_ref[:, :nk], kw_ref[...])
    v2_ref[...] = seg(C_V, C_U)
    cu_ref[...] = seg(C_GPRE, C_GATTN) * seg(C_U, C_GPOST)
    gc_ref[...] = (seg(C_GPOST, C_GPRE) * jax.nn.sigmoid(seg(C_GCONV, C_END))).astype(BF16)
    sa_ref[...] = jax.nn.sigmoid(seg(C_GATTN, C_GCONV)).astype(BF16)


def _project(x, sc1, sh1, nw, win2, qw, kw, hsum, hexp):
    nb, s, _ = x.shape
    rows = min(PROJ_ROWS, s)
    tok = lambda w, dt: jax.ShapeDtypeStruct((nb, s, w), dt)
    tspec = lambda w: pl.BlockSpec((None, rows, w), lambda b, i: (b, i, 0))
    return pl.pallas_call(
        _proj_kernel,
        out_shape=(tok(D_MODEL, BF16), tok(2 * KV_W, F32), tok(2 * KV_W, F32),
                   tok(D_MODEL, F32), tok(D_MODEL, BF16), tok(D_MODEL, BF16)),
        grid=(nb, s // rows),
        in_specs=[
            tspec(D_MODEL), _mod_spec(sc1, rows), _mod_spec(sh1, rows),
            _const_spec((1, D_MODEL)), _const_spec((D_MODEL, C_END)),
            _const_spec((1, D_MODEL)), _const_spec((1, 2 * KV_W)),
            _const_spec((D_MODEL, LANES)), _const_spec((LANES, D_MODEL)),
        ],
        out_specs=(tspec(D_MODEL), tspec(2 * KV_W), tspec(2 * KV_W),
                   tspec(D_MODEL), tspec(D_MODEL), tspec(D_MODEL)),
        compiler_params=_params("parallel", "parallel"),
        name="in_proj",
    )(x, sc1, sh1, nw, win2, qw, kw, hsum, hexp)


def _attend(qblk, k2, v2, sinks_ref, first_col):
    qb = qblk.shape[0]
    lane = lax.broadcasted_iota(jnp.int32, (qb, LANES), 1)
    low = lane < HEAD_DIM
    row = lax.broadcasted_iota(jnp.int32, (4 * qb, 2 * WINDOW), 0)
    col = lax.broadcasted_iota(jnp.int32, (4 * qb, 2 * WINDOW), 1)
    dist = col - (row & (qb - 1))
    mask = (dist >= 1) & (dist <= WINDOW) & (col >= first_col)
    blk = lax.broadcasted_iota(jnp.int32, (4 * qb, 1), 0) // qb
    ones = jnp.ones((2 * WINDOW, LANES), BF16)
    outs = []
    for j in range(N_KV_HEADS):
        kj = k2[:, j * LANES:(j + 1) * LANES]
        vj = v2[:, j * LANES:(j + 1) * LANES]
        qa = qblk[:, 2 * j * LANES:(2 * j + 1) * LANES].astype(F32)
        qc = qblk[:, (2 * j + 1) * LANES:(2 * j + 2) * LANES].astype(F32)
        lhs = jnp.concatenate([jnp.where(low, qa, 0.0), jnp.where(low, 0.0, qa),
                               jnp.where(low, qc, 0.0), jnp.where(low, 0.0, qc)], axis=0)
        s = jnp.where(mask, _dot_nt(lhs.astype(BF16), kj), -jnp.inf)
        sink = jnp.full((4 * qb, 1), sinks_ref[4 * j + 3], F32)
        for g in range(2, -1, -1):
            sink = jnp.where(blk == g, sinks_ref[4 * j + g], sink)
        m = jnp.maximum(jnp.max(s, axis=-1, keepdims=True), sink)
        p = jnp.exp(s - m).astype(BF16)
        o = _dot(p, jnp.concatenate([vj, ones], axis=1))
        on = o[:, :LANES] / (o[:, LANES:] + jnp.exp(sink - m))
        outs.append(jnp.where(low, on[0:qb], on[qb:2 * qb]))
        outs.append(jnp.where(low, on[2 * qb:3 * qb], on[3 * qb:4 * qb]))
    return jnp.concatenate(outs, axis=1)


def _merge_tail(x, cu, prev1, prev2, gc, sa, oattn, g1, sc2, sh2, cw_ref, cb, wout, n2w):
    conv_y = prev2 * cw_ref[0:1, :] + prev1 * cw_ref[1:2, :] + cu * cw_ref[2:3, :] + cb
    merged = sa.astype(F32) * oattn + gc.astype(F32) * conv_y
    x1 = x + g1 * _dot(merged.astype(BF16), wout)
    h2 = x1 * lax.rsqrt(jnp.mean(x1 * x1, axis=-1, keepdims=True) + RMS_EPS) * n2w
    h2 = h2 * (1.0 + sc2) + sh2
    return x1, _pack_words(h2)


def _prompt_mix_kernel(sinks_ref, x_ref, q_ref, k_ref, kp_ref, v_ref, vp_ref, cu_ref, cup_ref,
                       gc_ref, sa_ref, g1_ref, sc2_ref, sh2_ref, cw_ref, cb_ref, wout_ref, n2w_ref,
                       x1_ref, h2_ref, oat_scr):
    i = pl.program_id(1)
    rows = x_ref.shape[0]
    kext = jnp.concatenate([kp_ref[...], k_ref[...]], axis=0).astype(BF16)
    vext = jnp.concatenate([vp_ref[...], v_ref[...]], axis=0).astype(BF16)
    for w in range(rows // WINDOW):
        first_col = jnp.where(i == 0, WINDOW, 0) if w == 0 else 0
        lo = w * WINDOW
        oat_scr[lo:lo + WINDOW, :] = _attend(
            q_ref[lo:lo + WINDOW, :], kext[lo:lo + 2 * WINDOW, :], vext[lo:lo + 2 * WINDOW, :],
            sinks_ref, first_col)

    cu = cu_ref[...]
    has_prev = (i > 0).astype(F32)
    p1 = cup_ref[7:8, :] * has_prev
    p2 = cup_ref[6:7, :] * has_prev
    r = lax.broadcasted_iota(jnp.int32, (rows, 1), 0)
    prev1 = jnp.where(r == 0, p1, pltpu.roll(cu, 1, axis=0))
    prev2 = jnp.where(r == 0, p2, jnp.where(r == 1, p1, pltpu.roll(cu, 2, axis=0)))
    x1, h2 = _merge_tail(x_ref[...], cu, prev1, prev2, gc_ref[...], sa_ref[...], oat_scr[...],
                         g1_ref[...], sc2_ref[...], sh2_ref[...], cw_ref, cb_ref[...],
                         wout_ref[...], n2w_ref[...])
    x1_ref[...] = x1
    h2_ref[...] = h2


def _prompt_mix(sinks, x, q, k2, v2, cu, gc, sa, g1, sc2, sh2, cw, cb, wout, n2w):
    nb, s, _ = x.shape
    rows = ATTN_ROWS
    wpt = rows // WINDOW
    tspec = lambda w: pl.BlockSpec((None, rows, w), lambda b, i: (b, i, 0))
    prev_win = pl.BlockSpec((None, WINDOW, 2 * KV_W), lambda b, i: (b, jnp.maximum(i * wpt - 1, 0), 0))
    prev_rows = pl.BlockSpec((None, 8, D_MODEL), lambda b, i: (b, jnp.maximum(i * (rows // 8) - 1, 0), 0))
    mod = pl.BlockSpec((None, 1, D_MODEL), lambda b, i: (b, 0, 0))
    return pl.pallas_call(
        _prompt_mix_kernel,
        out_shape=(jax.ShapeDtypeStruct((nb, s, D_MODEL), F32),
                   jax.ShapeDtypeStruct((nb, s // 2, D_MODEL), jnp.uint32)),
        grid=(nb, s // rows),
        in_specs=[
            pl.BlockSpec(memory_space=pltpu.SMEM),
            tspec(D_MODEL), tspec(D_MODEL),
            tspec(2 * KV_W), prev_win, tspec(2 * KV_W), prev_win,
            tspec(D_MODEL), prev_rows, tspec(D_MODEL), tspec(D_MODEL),
            mod, mod, mod,
            _const_spec((CONV_W, D_MODEL)), _const_spec((1, D_MODEL)),
            _const_spec((D_MODEL, D_MODEL)), _const_spec((1, D_MODEL)),
        ],
        out_specs=(tspec(D_MODEL), pl.BlockSpec((None, rows // 2, D_MODEL), lambda b, i: (b, i, 0))),
        scratch_shapes=[pltpu.VMEM((rows, D_MODEL), F32)],
        compiler_params=_params("parallel", "parallel"),
        name="prompt_mix",
    )(sinks, x, q, k2, k2, v2, v2, cu, cu, gc, sa, g1, sc2, sh2, cw, cb, wout, n2w)


def _sample_attn_kernel(sinks_ref, q_ref, ck_ref, kn_ref, cv_ref, vn_ref, o_ref):
    pad = jnp.zeros((WINDOW - SAMPLE_Q_ROWS, 2 * KV_W), BF16)
    for n in range(q_ref.shape[0]):
        kext = jnp.concatenate([ck_ref[n], kn_ref[n], pad], axis=0)
        vext = jnp.concatenate([cv_ref[n], vn_ref[n], pad], axis=0)
        o_ref[n] = _attend(q_ref[n], kext, vext, sinks_ref, 0)


def _sample_attn(sinks, q8, ck2, k2n, cv2, v2n):
    nseq = q8.shape[0]
    sb = SAMPLE_SEQ_BLOCK
    spec = lambda r, w: pl.BlockSpec((sb, r, w), lambda n: (n, 0, 0))
    return pl.pallas_call(
        _sample_attn_kernel,
        out_shape=jax.ShapeDtypeStruct((nseq, SAMPLE_Q_ROWS, D_MODEL), F32),
        grid=(nseq // sb,),
        in_specs=[
            pl.BlockSpec(memory_space=pltpu.SMEM),
            spec(SAMPLE_Q_ROWS, D_MODEL),
            spec(WINDOW, 2 * KV_W), spec(SAMPLE_Q_ROWS, 2 * KV_W),
            spec(WINDOW, 2 * KV_W), spec(SAMPLE_Q_ROWS, 2 * KV_W),
        ],
        out_specs=spec(SAMPLE_Q_ROWS, D_MODEL),
        compiler_params=_params("parallel"),
        name="sample_attn",
    )(sinks, q8, ck2, k2n, cv2, v2n)


def _sample_mix_kernel(x_ref, cu_ref, p1_ref, p2_ref, gc_ref, sa_ref, oat_ref,
                       g1_ref, sc2_ref, sh2_ref, cw_ref, cb_ref, wout_ref, n2w_ref,
                       x1_ref, h2_ref):
    x1, h2 = _merge_tail(x_ref[...], cu_ref[...], p1_ref[...], p2_ref[...], gc_ref[...],
                         sa_ref[...], oat_ref[...], g1_ref[...], sc2_ref[...], sh2_ref[...],
                         cw_ref, cb_ref[...], wout_ref[...], n2w_ref[...])
    x1_ref[...] = x1
    h2_ref[...] = h2


def _sample_mix(x, cu, prev1, prev2, gc, sa, oattn, g1, sc2, sh2, cw, cb, wout, n2w):
    t = x.shape[0]
    full = lambda: pl.BlockSpec((t, D_MODEL), lambda: (0, 0))
    return pl.pallas_call(
        _sample_mix_kernel,
        out_shape=(jax.ShapeDtypeStruct((t, D_MODEL), F32),
                   jax.ShapeDtypeStruct((t // 2, D_MODEL), jnp.uint32)),
        in_specs=[full() for _ in range(10)] + [
            pl.BlockSpec((CONV_W, D_MODEL), lambda: (0, 0)),
            pl.BlockSpec((1, D_MODEL), lambda: (0, 0)),
            pl.BlockSpec((D_MODEL, D_MODEL), lambda: (0, 0)),
            pl.BlockSpec((1, D_MODEL), lambda: (0, 0)),
        ],
        out_specs=(full(), pl.BlockSpec((t // 2, D_MODEL), lambda: (0, 0))),
        compiler_params=pltpu.CompilerParams(vmem_limit_bytes=VMEM_LIMIT),
        name="sample_mix",
    )(x, cu, prev1, prev2, gc, sa, oattn, g1, sc2, sh2, cw, cb, wout, n2w)


def _top16(s, index):
    rank = jnp.full(s.shape, float(PEER_TOPK), F32)
    vals = []
    for r in range(PEER_TOPK):
        m = jnp.max(s, axis=0, keepdims=True)
        first = jnp.min(jnp.where(s == m, index, float(s.shape[0])), axis=0, keepdims=True)
        hit = index == first
        rank = jnp.where(hit, float(r), rank)
        s = jnp.where(hit, -jnp.inf, s)
        vals.append(m)
    return vals, rank


def _row_array(rows, index):
    out = jnp.zeros(index.shape, F32)
    for r, v in enumerate(rows):
        out = jnp.where(index == float(r), v, out)
    return out


def _gates(s1, s2, v1_0, v2_0, z):
    return jnp.exp(s1 - v1_0) / z, jnp.exp(s2 - v2_0)


def _route_exact(s1, s2, idx16, idx128, idx256):
    v1, rank1 = _top16(s1, idx128)
    v2, rank2 = _top16(s2, idx128)
    v2a = _row_array(v2, idx16)
    cand = jnp.concatenate([v1[r] + v2a for r in range(PEER_TOPK)], axis=0)
    sel = jnp.zeros(cand.shape, F32)
    c = cand
    for _ in range(PEER_TOPK):
        m = jnp.max(c, axis=0, keepdims=True)
        first = jnp.min(jnp.where(c == m, idx256, float(PEER_TOPK * PEER_TOPK)),
                        axis=0, keepdims=True)
        hit = idx256 == first
        sel = jnp.where(hit, 1.0, sel)
        c = jnp.where(hit, -jnp.inf, c)
    z = jnp.sum(sel * jnp.exp(cand - (v1[0] + v2[0])), axis=0, keepdims=True)
    n1 = jnp.zeros(rank1.shape, F32)
    for r in range(PEER_TOPK):
        n_r = jnp.sum(sel[r * PEER_TOPK:(r + 1) * PEER_TOPK], axis=0, keepdims=True)
        n1 = jnp.where(rank1 == float(r), n_r, n1)
    e1, e2 = _gates(s1, s2, v1[0], v2[0], z)
    return n1, e1, rank2, e2


def _top16_distinct(s):
    rank = jnp.full(s.shape, float(PEER_TOPK), F32)
    vals = []
    for r in range(PEER_TOPK):
        m = jnp.max(s, axis=0, keepdims=True)
        hit = s == m
        rank = jnp.where(hit, float(r), rank)
        s = jnp.where(hit, -jnp.inf, s)
        vals.append(m)
    ranked = jnp.sum(jnp.where(rank < float(PEER_TOPK), 1.0, 0.0), axis=0, keepdims=True)
    return vals, rank, ranked


def _route_distinct(s1, s2, idx8, idx16):
    v1, rank1, ranked1 = _top16_distinct(s1)
    v2, rank2, ranked2 = _top16_distinct(s2)
    v1a = _row_array(v1, idx16)
    v2a = _row_array(v2, idx16)
    v2lo = v2a[0:8]
    groups = [v1[0] + v2a, v1[1] + v2lo]
    for r1 in range(2, 8):
        groups.append(jnp.where(idx8 < float(PEER_TOPK // (r1 + 1)), v1[r1] + v2lo, -jnp.inf))
    groups.append(v1a[8:16] + v2[0])
    cand = jnp.concatenate(groups, axis=0)
    c = cand
    for _ in range(PEER_TOPK):
        m = jnp.max(c, axis=0, keepdims=True)
        c = jnp.where(c == m, -jnp.inf, c)
    sel = jnp.where(cand >= m, 1.0, 0.0)
    z = jnp.sum(sel * jnp.exp(cand - (v1[0] + v2[0])), axis=0, keepdims=True)
    counts = [jnp.sum(sel[0:16], axis=0, keepdims=True)]
    counts += [jnp.sum(sel[8 * r1 + 8:8 * r1 + 16], axis=0, keepdims=True) for r1 in range(1, 8)]
    counts += [sel[72 + k:73 + k] for k in range(8)]
    total = counts[0]
    for n_r in counts[1:]:
        total = total + n_r
    n1 = jnp.zeros(rank1.shape, F32)
    for r in range(PEER_TOPK):
        n1 = jnp.where(rank1 == float(r), counts[r], n1)
    e1, e2 = _gates(s1, s2, v1[0], v2[0], z)
    want = float(PEER_TOPK)
    valid = (ranked1 == want) & (ranked2 == want) & (total == want)
    return n1, e1, rank2, e2, valid


def _route_kernel(h2_ref, wqt_ref, keys_ref, n1_ref, e1_ref, r2_ref, e2_ref, qpt_scr):
    rows = 2 * h2_ref.shape[0]
    qpt_scr[...] = _dot_nt(wqt_ref[...], _unpack_words(h2_ref[...]))
    idx8 = lax.broadcasted_iota(jnp.int32, (8, LANES), 0).astype(F32)
    idx16 = lax.broadcasted_iota(jnp.int32, (PEER_TOPK, LANES), 0).astype(F32)

    groups = PEER_HEADS // ROUTE_HEADS

    def body(it, carry):
        tc = it // groups
        ls = pl.ds(pl.multiple_of(tc * LANES, LANES), LANES)
        hp = lax.Precision.HIGHEST
        heads, scores = [], []
        for j in range(ROUTE_HEADS):
            h = (it % groups) * ROUTE_HEADS + j
            base = pl.multiple_of(h * (2 * PEER_HALF), 2 * PEER_HALF)
            q1 = qpt_scr[pl.ds(base, PEER_HALF), ls]
            q2 = qpt_scr[pl.ds(base + PEER_HALF, PEER_HALF), ls]
            s1 = jnp.dot(keys_ref[2 * h], q1, precision=hp, preferred_element_type=F32)
            s2 = jnp.dot(keys_ref[2 * h + 1], q2, precision=hp, preferred_element_type=F32)
            heads.append(h)
            scores.append((s1, s2))

        def store(h, n1, e1, rank2, e2):
            n1_ref[h, :, ls] = n1
            e1_ref[h, :, ls] = e1
            r2_ref[h, :, ls] = _pack_words(rank2)
            e2_ref[h, :, ls] = _pack_words(e2)

        ties = jnp.zeros((1, LANES), F32)
        for h, (s1, s2) in zip(heads, scores):
            n1, e1, rank2, e2, valid = _route_distinct(s1, s2, idx8, idx16)
            store(h, n1, e1, rank2, e2)
            ties = jnp.where(valid, ties, 1.0)

        @pl.when(jnp.max(ties) > 0.0)
        def _():
            idx128 = lax.broadcasted_iota(jnp.int32, (PEER_KEYS, LANES), 0).astype(F32)
            idx256 = lax.broadcasted_iota(jnp.int32, (PEER_TOPK * PEER_TOPK, LANES), 0).astype(F32)
            for h, (s1, s2) in zip(heads, scores):
                store(h, *_route_exact(s1, s2, idx16, idx128, idx256))

        return carry

    lax.fori_loop(0, (rows // LANES) * groups, body, 0)


def _route(h2, wqt, keys):
    t = 2 * h2.shape[0]
    rows = min(ROUTE_ROWS, t)
    out = jax.ShapeDtypeStruct((PEER_HEADS, PEER_KEYS, t), F32)
    outw = jax.ShapeDtypeStruct((PEER_HEADS, PEER_KEYS // 2, t), jnp.uint32)
    ospec = pl.BlockSpec((PEER_HEADS, PEER_KEYS, rows), lambda i: (0, 0, i))
    wspec = pl.BlockSpec((PEER_HEADS, PEER_KEYS // 2, rows), lambda i: (0, 0, i))
    return pl.pallas_call(
        _route_kernel,
        out_shape=(out, out, outw, outw),
        grid=(t // rows,),
        in_specs=[
            pl.BlockSpec((rows // 2, D_MODEL), lambda i: (i, 0)),
            _const_spec((D_MODEL, D_MODEL)),
            _const_spec((2 * PEER_HEADS, PEER_KEYS, PEER_HALF)),
        ],
        out_specs=(ospec, ospec, wspec, wspec),
        scratch_shapes=[pltpu.VMEM((D_MODEL, rows), F32)],
        compiler_params=_params("parallel"),
        name="peer_route",
    )(h2, wqt, keys)


PACK = 16


def _expert_weights(n1_ref, e1_ref, r2_ref, e2_ref, at_ref, hw_ref, tc, il):
    ls = slice(tc * LANES, (tc + 1) * LANES)
    gates = [None] * (PEER_KEYS // PACK)
    for h in range(PEER_HEADS):
        nb = jnp.broadcast_to(n1_ref[h, il:il + 1, ls], (PACK, LANES)).astype(BF16)
        eb = jnp.broadcast_to(e1_ref[h, il:il + 1, ls], (PACK, LANES)).astype(BF16)
        for k in range(PEER_KEYS // PACK):
            ws = slice(PACK // 2 * k, PACK // 2 * (k + 1))
            e2 = _unpack_words(e2_ref[h, ws, ls])
            term = jnp.where(_unpack_words(r2_ref[h, ws, ls]) < nb, e2, jnp.zeros_like(e2)) * eb
            gates[k] = term if h == 0 else gates[k] + term
    for k in range(PEER_KEYS // PACK):
        e0 = il * PEER_KEYS + PACK * k
        a = at_ref[e0:e0 + PACK, ls]
        act = 0.5 * a * (1.0 + lax.erf(a * (2.0 ** -0.5)))
        hw_ref[e0 // 2:(e0 + PACK) // 2, ls] = pltpu.bitcast(gates[k] * act.astype(BF16), jnp.uint32)


MXU_COLS = 256


def _peer_kernel(h2_ref, u_ref, vt_ref,
                 n1p_ref, e1p_ref, r2p_ref, e2p_ref, n1c_ref, e1c_ref, r2c_ref, e2c_ref,
                 x1_ref, g2_ref, y_ref, at0_scr, at1_scr, hw0_scr, hw1_scr, acc_scr,
                 *, pairs_per_block):
    g = pl.program_id(0)
    ec = PEER_CHUNK
    rows = x1_ref.shape[0]

    @pl.when(g == 0)
    def _():
        at1_scr[...] = jnp.zeros_like(at1_scr)
        hw0_scr[...] = jnp.zeros_like(hw0_scr)
        hw1_scr[...] = jnp.zeros_like(hw1_scr)
        acc_scr[...] = jnp.zeros_like(acc_scr)

    pair = jnp.maximum(g - 1, 0) % pairs_per_block

    def half_step(uw_row0, routing, at_w, at_r, hw_w, hw_r, vtw_col0, restart):
        for n in range(rows // MXU_COLS):
            ts = slice(n * MXU_COLS, (n + 1) * MXU_COLS)
            tw = slice(n * MXU_COLS // 2, (n + 1) * MXU_COLS // 2)
            at_w[:, ts] = _dot_nt(_unpack_words(u_ref[uw_row0:uw_row0 + ec // 2, :]),
                                  _unpack_words(h2_ref[tw, :]))
            for tc in range(n * (MXU_COLS // LANES), (n + 1) * (MXU_COLS // LANES)):
                for il in range(ec // PEER_KEYS):
                    _expert_weights(*routing, at_r, hw_w, tc, il)
            acc = acc_scr[:, ts]
            if restart is not None:
                acc = jnp.where(restart, 0.0, acc)
            acc_scr[:, ts] = acc + _dot(_unpack_words(vt_ref[:, vtw_col0:vtw_col0 + ec]),
                                        _unpack_words(hw_r[:, ts]))

    half_step(0, (n1p_ref, e1p_ref, r2p_ref, e2p_ref), at0_scr, at1_scr,
              hw1_scr, hw0_scr, 0, pair == 0)
    half_step(ec // 2, (n1c_ref, e1c_ref, r2c_ref, e2c_ref), at1_scr, at0_scr,
              hw0_scr, hw1_scr, ec, None)

    @pl.when((g > 0) & (pair == pairs_per_block - 1))
    def _():
        y_ref[...] = x1_ref[...] + g2_ref[...] * acc_scr[...].T


def _peer(h2, u, vt, n1, e1, r2, e2, x1, g2):
    t = x1.shape[0]
    rows = min(PEER_ROWS, t)
    n_chunks = PEER_EXPERTS // PEER_CHUNK
    ppb = n_chunks // 2
    n_pairs = (t // rows) * ppb
    n_items = 2 * n_pairs
    il_rows = PEER_CHUNK // PEER_KEYS

    pair_a = lambda g: jnp.minimum(g, n_pairs - 1)
    item_p = lambda g: jnp.maximum(2 * g - 1, 0)
    item_c = lambda g: jnp.minimum(2 * g, n_items - 1)
    block_c = lambda g: jnp.maximum(g - 1, 0) // ppb

    def routing_specs(item):
        per_chunk = pl.BlockSpec((PEER_HEADS, il_rows, rows),
                                 lambda g: (0, item(g) % n_chunks, item(g) // n_chunks))
        per_block = pl.BlockSpec((PEER_HEADS, PEER_KEYS // 2, rows),
                                 lambda g: (0, 0, item(g) // n_chunks))
        return [per_chunk, per_chunk, per_block, per_block]

    if g2.shape[1] == 1:
        blocks_per_seq = (t // g2.shape[0]) // rows
        g2_spec = pl.BlockSpec((None, 1, D_MODEL), lambda g: (block_c(g) // blocks_per_seq, 0, 0))
    else:
        g2_spec = pl.BlockSpec((None, rows, D_MODEL), lambda g: (0, block_c(g), 0))
    tok_c = pl.BlockSpec((rows, D_MODEL), lambda g: (block_c(g), 0))
    return pl.pallas_call(
        functools.partial(_peer_kernel, pairs_per_block=ppb),
        out_shape=jax.ShapeDtypeStruct((t, D_MODEL), F32),
        grid=(n_pairs + 1,),
        in_specs=[
            pl.BlockSpec((rows // 2, D_MODEL), lambda g: (pair_a(g) // ppb, 0)),
            pl.BlockSpec((PEER_CHUNK, D_MODEL), lambda g: (pair_a(g) % ppb, 0)),
            pl.BlockSpec((D_MODEL // 2, 2 * PEER_CHUNK), lambda g: (0, jnp.maximum(g - 1, 0) % ppb)),
            *routing_specs(item_p), *routing_specs(item_c),
            tok_c, g2_spec,
        ],
        out_specs=tok_c,
        scratch_shapes=[pltpu.VMEM((PEER_CHUNK, rows), F32),
                        pltpu.VMEM((PEER_CHUNK, rows), F32),
                        pltpu.VMEM((PEER_CHUNK // 2, rows), jnp.uint32),
                        pltpu.VMEM((PEER_CHUNK // 2, rows), jnp.uint32),
                        pltpu.VMEM((D_MODEL, rows), F32)],
        compiler_params=_params("arbitrary"),
        name="peer_experts",
    )(h2, u, vt, n1, e1, r2, e2, n1, e1, r2, e2, x1, g2)


def _dup_heads(w):
    lead = w.shape[:-1]
    w = w.reshape(lead + (N_KV_HEADS, 1, HEAD_DIM))
    return jnp.broadcast_to(w, lead + (N_KV_HEADS, 2, HEAD_DIM)).reshape(lead + (2 * KV_W,))


def _first_of_dup(a):
    return a.reshape(a.shape[:-1] + (N_KV_HEADS, 2, HEAD_DIM))[..., 0, :]


def kernel(x_prompt, x_sample, cache_k, cache_v, state_conv, c_prompt, c_sample, w_ada, b_ada, norm1_w, w_in, q_norm_w, k_norm_w, sinks, conv_w, conv_bias, w_out, norm2_w, w_query, sub_keys, expert_u, expert_v):
    depth = w_ada.shape[0]
    batch, seq, _ = x_prompt.shape
    nseq, dec = x_sample.shape[:2]
    assert dec <= SAMPLE_Q_ROWS and dec >= CONV_W - 1
    ts = nseq * dec

    head_of_col = jnp.arange(D_MODEL, dtype=jnp.int32) // HEAD_DIM
    hsum = (head_of_col[:, None] == jnp.arange(LANES, dtype=jnp.int32)[None, :]).astype(BF16)
    hexp = hsum.T
    c_all = jnp.concatenate([c_prompt, c_sample], axis=0)
    c_all = jnp.pad(c_all, ((0, (-c_all.shape[0]) % 8), (0, 0)))

    yp = x_prompt
    ys = x_sample.reshape(1, ts, D_MODEL)
    kp, vp, cp, ksm, vsm, csm = [], [], [], [], [], []
    for l in range(depth):
        win = w_in[l]
        win2 = jnp.concatenate(
            [win[:, :D_MODEL], _dup_heads(win[:, D_MODEL:D_MODEL + KV_W]),
             _dup_heads(win[:, D_MODEL + KV_W:D_MODEL + 2 * KV_W]), win[:, D_MODEL + 2 * KV_W:]],
            axis=1).astype(BF16)
        nw1 = norm1_w[l].reshape(1, D_MODEL)
        nw2 = norm2_w[l].reshape(1, D_MODEL)
        qw = jnp.tile(q_norm_w[l], N_HEADS).reshape(1, D_MODEL)
        kw = jnp.tile(k_norm_w[l], 2 * N_KV_HEADS).reshape(1, 2 * KV_W)
        cw = conv_w[l]
        cb = conv_bias[l].reshape(1, D_MODEL)
        wout = w_out[l].astype(BF16)
        wqt = w_query[l].T.astype(BF16)
        keys = sub_keys[l].reshape(2 * PEER_HEADS, PEER_KEYS, PEER_HALF)
        u = _pack_table(expert_u[l])
        vt = _pack_table_transposed(expert_v[l])

        mod = _adaln(c_all, w_ada[l], b_ada[l])
        mod_p = [mod[j, :batch].reshape(batch, 1, D_MODEL) for j in range(6)]
        mod_s = [jnp.repeat(mod[j, batch:batch + nseq], dec, axis=0).reshape(1, ts, D_MODEL)
                 for j in range(6)]

        q, k2, v2, cu, gc, sa = _project(yp, mod_p[1], mod_p[0], nw1, win2, qw, kw, hsum, hexp)
        x1, h2 = _prompt_mix(sinks[l], yp, q, k2, v2, cu, gc, sa, mod_p[2], mod_p[4], mod_p[3],
                             cw, cb, wout, nw2)
        h2 = h2.reshape(batch * seq // 2, D_MODEL)
        n1, e1, r2, e2 = _route(h2, wqt, keys)
        yp = _peer(h2, u, vt, n1, e1, r2, e2, x1.reshape(batch * seq, D_MODEL), mod_p[5])
        yp = yp.reshape(batch, seq, D_MODEL)
        kp.append(_first_of_dup(k2[:, seq - WINDOW:]))
        vp.append(_first_of_dup(v2[:, seq - WINDOW:]))
        cp.append(cu[:, seq - (CONV_W - 1):])

        q, k2, v2, cu, gc, sa = _project(ys, mod_s[1], mod_s[0], nw1, win2, qw, kw, hsum, hexp)
        pad_rows = lambda a: jnp.pad(a.reshape(nseq, dec, a.shape[-1]),
                                     ((0, 0), (0, SAMPLE_Q_ROWS - dec), (0, 0)))
        oat = _sample_attn(sinks[l], pad_rows(q[0]),
                           _dup_heads(cache_k[l].reshape(nseq, WINDOW, KV_W)).astype(BF16),
                           pad_rows(k2[0]).astype(BF16),
                           _dup_heads(cache_v[l].reshape(nseq, WINDOW, KV_W)).astype(BF16),
                           pad_rows(v2[0]).astype(BF16))
        oat = oat[:, :dec].reshape(ts, D_MODEL)
        cu_seq = cu.reshape(nseq, dec, D_MODEL)
        up = jnp.concatenate([state_conv[l], cu_seq], axis=1)
        prev2 = up[:, 0:dec].reshape(ts, D_MODEL)
        prev1 = up[:, 1:dec + 1].reshape(ts, D_MODEL)
        flat = lambda a: a.reshape(ts, D_MODEL)
        x1, h2 = _sample_mix(flat(ys), flat(cu), prev1, prev2, flat(gc), flat(sa), oat,
                             flat(mod_s[2]), flat(mod_s[4]), flat(mod_s[3]), cw, cb, wout, nw2)
        n1, e1, r2, e2 = _route(h2, wqt, keys)
        ys = _peer(h2, u, vt, n1, e1, r2, e2, x1, mod_s[5]).reshape(1, ts, D_MODEL)
        k_new = _first_of_dup(k2[0]).reshape(nseq, dec, N_KV_HEADS, HEAD_DIM)
        v_new = _first_of_dup(v2[0]).reshape(nseq, dec, N_KV_HEADS, HEAD_DIM)
        ksm.append(jnp.concatenate([cache_k[l], k_new], axis=1)[:, -WINDOW:])
        vsm.append(jnp.concatenate([cache_v[l], v_new], axis=1)[:, -WINDOW:])
        csm.append(up[:, -(CONV_W - 1):])

    return (yp, ys.reshape(nseq, dec, D_MODEL), jnp.stack(kp), jnp.stack(vp), jnp.stack(cp),
            jnp.stack(ksm), jnp.stack(vsm), jnp.stack(csm))
```

```python
import functools

import jax
import jax.numpy as jnp
from jax import lax
from jax.experimental import pallas as pl
from jax.experimental.pallas import tpu as pltpu

F32 = jnp.float32
BF16 = jnp.bfloat16

D_MODEL = 1024
N_HEADS = 16
N_KV_HEADS = 4
HEAD_DIM = 64
WINDOW = 128
KV_W = N_KV_HEADS * HEAD_DIM
CONV_W = 3
PEER_HEADS = 8
PEER_KEYS = 128
PEER_TOPK = 16
PEER_HALF = 64
PEER_EXPERTS = PEER_KEYS * PEER_KEYS
RMS_EPS = 1e-6

LANES = 128
VMEM_LIMIT = 56 * 1024 * 1024

PROJ_ROWS = 512
ATTN_ROWS = 512
PACK_ROWS = 1024
ROUTE_ROWS = 512
ROUTE_HEADS = 2
PEER_ROWS = 512
PEER_CHUNK = 1024
SAMPLE_Q_ROWS = 8
SAMPLE_SEQ_BLOCK = 8

C_Q = 0
C_K = C_Q + D_MODEL
C_V = C_K + 2 * KV_W
C_U = C_V + 2 * KV_W
C_GPOST = C_U + D_MODEL
C_GPRE = C_GPOST + D_MODEL
C_GATTN = C_GPRE + D_MODEL
C_GCONV = C_GATTN + D_MODEL
C_END = C_GCONV + D_MODEL


def _params(*semantics):
    return pltpu.CompilerParams(dimension_semantics=semantics, vmem_limit_bytes=VMEM_LIMIT)


def _const_spec(shape):
    nd = len(shape)
    return pl.BlockSpec(shape, lambda *_: (0,) * nd, pipeline_mode=pl.Buffered(1))


def _mod_spec(arr, rows):
    if arr.shape[1] == 1:
        return pl.BlockSpec((None, 1, D_MODEL), lambda b, i, *_: (b, 0, 0))
    return pl.BlockSpec((None, rows, D_MODEL), lambda b, i, *_: (b, i, 0))


def _dot(a, b):
    return jnp.dot(a, b, preferred_element_type=F32)


def _dot_nt(a, b):
    return lax.dot_general(a, b, (((1,), (1,)), ((), ())), preferred_element_type=F32)


def _pack_words(x):
    return pltpu.bitcast(x.astype(BF16), jnp.uint32)


def _unpack_words(w):
    return pltpu.bitcast(w, BF16)


def _dot_split(a, b):
    hi = a.astype(BF16)
    lo = (a - hi.astype(F32)).astype(BF16)
    return _dot(hi, b) + _dot(lo, b)


def _ada_kernel(c_ref, w_ref, b_ref, o_ref):
    s = jax.nn.silu(c_ref[...])
    o_ref[...] = jnp.dot(s, w_ref[...], precision=lax.Precision.HIGHEST,
                         preferred_element_type=F32) + b_ref[...]


def _adaln(c_all, w_ada, b_ada):
    rows = c_all.shape[0]
    return pl.pallas_call(
        _ada_kernel,
        out_shape=jax.ShapeDtypeStruct((6, rows, D_MODEL), F32),
        grid=(6,),
        in_specs=[
            pl.BlockSpec((rows, D_MODEL), lambda j: (0, 0)),
            pl.BlockSpec((D_MODEL, D_MODEL), lambda j: (0, j)),
            pl.BlockSpec((1, D_MODEL), lambda j: (0, j)),
        ],
        out_specs=pl.BlockSpec((None, rows, D_MODEL), lambda j: (j, 0, 0)),
        compiler_params=_params("arbitrary"),
        name="adaln",
    )(c_all, w_ada, b_ada.reshape(1, 6 * D_MODEL))


def _pack_kernel(x_ref, o_ref):
    o_ref[...] = _pack_words(x_ref[...])


def _pack_t_kernel(x_ref, o_ref):
    o_ref[...] = _pack_words(x_ref[...].T)


def _pack_table(x):
    n, m = x.shape
    return pl.pallas_call(
        _pack_kernel,
        out_shape=jax.ShapeDtypeStruct((n // 2, m), jnp.uint32),
        grid=(n // PACK_ROWS,),
        in_specs=[pl.BlockSpec((PACK_ROWS, m), lambda i: (i, 0))],
        out_specs=pl.BlockSpec((PACK_ROWS // 2, m), lambda i: (i, 0)),
        compiler_params=_params("parallel"),
        name="pack_table",
    )(x)


def _pack_table_transposed(x):
    n, m = x.shape
    return pl.pallas_call(
        _pack_t_kernel,
        out_shape=jax.ShapeDtypeStruct((m // 2, n), jnp.uint32),
        grid=(n // PACK_ROWS,),
        in_specs=[pl.BlockSpec((PACK_ROWS, m), lambda i: (i, 0))],
        out_specs=pl.BlockSpec((m // 2, PACK_ROWS), lambda i: (0, i)),
        compiler_params=_params("parallel"),
        name="pack_table_t",
    )(x)


def _head_rms(z, hsum, hexp, w):
    ss = _dot_split(z * z, hsum)
    r = lax.rsqrt(ss * (1.0 / HEAD_DIM) + RMS_EPS)
    return z * _dot_split(r, hexp) * w


def _proj_kernel(x_ref, sc_ref, sh_ref, nw_ref, win_ref, qw_ref, kw_ref, hsum_ref, hexp_ref,
                 q_ref, k2_ref, v2_ref, cu_ref, gc_ref, sa_ref):
    x = x_ref[...]
    h = x * lax.rsqrt(jnp.mean(x * x, axis=-1, keepdims=True) + RMS_EPS) * nw_ref[...]
    h = h * (1.0 + sc_ref[...]) + sh_ref[...]
    hb = h.astype(BF16)

    def seg(lo, hi):
        return _dot(hb, win_ref[:, lo:hi])

    q = _head_rms(seg(C_Q, C_K), hsum_ref[...], hexp_ref[...], qw_ref[...])
    q_ref[...] = (q * (HEAD_DIM ** -0.5)).astype(BF16)
    nk = 2 * KV_W
    k2_ref[...] = _head_rms(seg(C_K, C_V), hsum_ref[:nk, :], hexp_ref[:, :nk], kw_ref[...])
    v2_ref[...] = seg(C_V, C_U)
    cu_ref[...] = seg(C_GPRE, C_GATTN) * seg(C_U, C_GPOST)
    gc_ref[...] = (seg(C_GPOST, C_GPRE) * jax.nn.sigmoid(seg(C_GCONV, C_END))).astype(BF16)
    sa_ref[...] = jax.nn.sigmoid(seg(C_GATTN, C_GCONV)).astype(BF16)


def _project(x, sc1, sh1, nw, win2, qw, kw, hsum, hexp):
    nb, s, _ = x.shape
    rows = min(PROJ_ROWS, s)
    tok = lambda w, dt: jax.ShapeDtypeStruct((nb, s, w), dt)
    tspec = lambda w: pl.BlockSpec((None, rows, w), lambda b, i: (b, i, 0))
    return pl.pallas_call(
        _proj_kernel,
        out_shape=(tok(D_MODEL, BF16), tok(2 * KV_W, F32), tok(2 * KV_W, F32),
                   tok(D_MODEL, F32), tok(D_MODEL, BF16), tok(D_MODEL, BF16)),
        grid=(nb, s // rows),
        in_specs=[
            tspec(D_MODEL), _mod_spec(sc1, rows), _mod_spec(sh1, rows),
            _const_spec((1, D_MODEL)), _const_spec((D_MODEL, C_END)),
            _const_spec((1, D_MODEL)), _const_spec((1, 2 * KV_W)),
            _const_spec((D_MODEL, LANES)), _const_spec((LANES, D_MODEL)),
        ],
        out_specs=(tspec(D_MODEL), tspec(2 * KV_W), tspec(2 * KV_W),
                   tspec(D_MODEL), tspec(D_MODEL), tspec(D_MODEL)),
        compiler_params=_params("parallel", "parallel"),
        name="in_proj",
    )(x, sc1, sh1, nw, win2, qw, kw, hsum, hexp)


def _attend_blocks(blocks, sinks_ref):
    qb = blocks[0][0].shape[0]
    lane = lax.broadcasted_iota(jnp.int32, (qb, LANES), 1)
    low = lane < HEAD_DIM
    row = lax.broadcasted_iota(jnp.int32, (4 * qb, 2 * WINDOW), 0)
    col = lax.broadcasted_iota(jnp.int32, (4 * qb, 2 * WINDOW), 1)
    dist = col - (row & (qb - 1))
    band = (dist >= 1) & (dist <= WINDOW)
    blk = lax.broadcasted_iota(jnp.int32, (4 * qb, 1), 0) // qb
    ones = jnp.ones((2 * WINDOW, LANES), BF16)
    sinks = []
    for j in range(N_KV_HEADS):
        sink = jnp.full((4 * qb, 1), sinks_ref[4 * j + 3], F32)
        for g in range(2, -1, -1):
            sink = jnp.where(blk == g, sinks_ref[4 * j + g], sink)
        sinks.append(sink)

    scores = []
    for qblk, k2, _, _ in blocks:
        for j in range(N_KV_HEADS):
            qa = qblk[:, 2 * j * LANES:(2 * j + 1) * LANES].astype(F32)
            qc = qblk[:, (2 * j + 1) * LANES:(2 * j + 2) * LANES].astype(F32)
            lhs = jnp.concatenate([jnp.where(low, qa, 0.0), jnp.where(low, 0.0, qa),
                                   jnp.where(low, qc, 0.0), jnp.where(low, 0.0, qc)], axis=0)
            scores.append(_dot_nt(lhs.astype(BF16), k2[:, j * LANES:(j + 1) * LANES]))

    probs = []
    for b, (_, _, _, first_col) in enumerate(blocks):
        mask = band if isinstance(first_col, int) and first_col == 0 else band & (col >= first_col)
        for j in range(N_KV_HEADS):
            s = jnp.where(mask, scores[b * N_KV_HEADS + j], -jnp.inf)
            m = jnp.maximum(jnp.max(s, axis=-1, keepdims=True), sinks[j])
            probs.append((jnp.exp(s - m).astype(BF16), jnp.exp(sinks[j] - m)))

    results = []
    for b, (_, _, v2, _) in enumerate(blocks):
        outs = []
        for j in range(N_KV_HEADS):
            p, sink_term = probs[b * N_KV_HEADS + j]
            o = _dot(p, jnp.concatenate([v2[:, j * LANES:(j + 1) * LANES], ones], axis=1))
            on = o[:, :LANES] / (o[:, LANES:] + sink_term)
            outs.append(jnp.where(low, on[0:qb], on[qb:2 * qb]))
            outs.append(jnp.where(low, on[2 * qb:3 * qb], on[3 * qb:4 * qb]))
        results.append(jnp.concatenate(outs, axis=1))
    return results


def _merge_tail(x, cu, prev1, prev2, gc, sa, oattn, g1, sc2, sh2, cw_ref, cb, wout, n2w):
    conv_y = prev2 * cw_ref[0:1, :] + prev1 * cw_ref[1:2, :] + cu * cw_ref[2:3, :] + cb
    merged = sa.astype(F32) * oattn + gc.astype(F32) * conv_y
    x1 = x + g1 * _dot(merged.astype(BF16), wout)
    h2 = x1 * lax.rsqrt(jnp.mean(x1 * x1, axis=-1, keepdims=True) + RMS_EPS) * n2w
    h2 = h2 * (1.0 + sc2) + sh2
    return x1, _pack_words(h2)


def _prompt_mix_kernel(sinks_ref, x_ref, q_ref, k_ref, kp_ref, v_ref, vp_ref, cu_ref, cup_ref,
                       gc_ref, sa_ref, g1_ref, sc2_ref, sh2_ref, cw_ref, cb_ref, wout_ref, n2w_ref,
                       x1_ref, h2_ref, oat_scr):
    i = pl.program_id(1)
    rows = x_ref.shape[0]
    kext = jnp.concatenate([kp_ref[...], k_ref[...]], axis=0).astype(BF16)
    vext = jnp.concatenate([vp_ref[...], v_ref[...]], axis=0).astype(BF16)
    for w in range(rows // WINDOW):
        first_col = jnp.where(i == 0, WINDOW, 0) if w == 0 else 0
        lo = w * WINDOW
        oat_scr[lo:lo + WINDOW, :] = _attend_blocks(
            [(q_ref[lo:lo + WINDOW, :], kext[lo:lo + 2 * WINDOW, :], vext[lo:lo + 2 * WINDOW, :],
              first_col)], sinks_ref)[0]

    cu = cu_ref[...]
    has_prev = (i > 0).astype(F32)
    p1 = cup_ref[7:8, :] * has_prev
    p2 = cup_ref[6:7, :] * has_prev
    r = lax.broadcasted_iota(jnp.int32, (rows, 1), 0)
    prev1 = jnp.where(r == 0, p1, pltpu.roll(cu, 1, axis=0))
    prev2 = jnp.where(r == 0, p2, jnp.where(r == 1, p1, pltpu.roll(cu, 2, axis=0)))
    x1, h2 = _merge_tail(x_ref[...], cu, prev1, prev2, gc_ref[...], sa_ref[...], oat_scr[...],
                         g1_ref[...], sc2_ref[...], sh2_ref[...], cw_ref, cb_ref[...],
                         wout_ref[...], n2w_ref[...])
    x1_ref[...] = x1
    h2_ref[...] = h2


def _prompt_mix(sinks, x, q, k2, v2, cu, gc, sa, g1, sc2, sh2, cw, cb, wout, n2w):
    nb, s, _ = x.shape
    rows = ATTN_ROWS
    wpt = rows // WINDOW
    tspec = lambda w: pl.BlockSpec((None, rows, w), lambda b, i: (b, i, 0))
    prev_win = pl.BlockSpec((None, WINDOW, 2 * KV_W), lambda b, i: (b, jnp.maximum(i * wpt - 1, 0), 0))
    prev_rows = pl.BlockSpec((None, 8, D_MODEL), lambda b, i: (b, jnp.maximum(i * (rows // 8) - 1, 0), 0))
    mod = pl.BlockSpec((None, 1, D_MODEL), lambda b, i: (b, 0, 0))
    return pl.pallas_call(
        _prompt_mix_kernel,
        out_shape=(jax.ShapeDtypeStruct((nb, s, D_MODEL), F32),
                   jax.ShapeDtypeStruct((nb, s // 2, D_MODEL), jnp.uint32)),
        grid=(nb, s // rows),
        in_specs=[
            pl.BlockSpec(memory_space=pltpu.SMEM),
            tspec(D_MODEL), tspec(D_MODEL),
            tspec(2 * KV_W), prev_win, tspec(2 * KV_W), prev_win,
            tspec(D_MODEL), prev_rows, tspec(D_MODEL), tspec(D_MODEL),
            mod, mod, mod,
            _const_spec((CONV_W, D_MODEL)), _const_spec((1, D_MODEL)),
            _const_spec((D_MODEL, D_MODEL)), _const_spec((1, D_MODEL)),
        ],
        out_specs=(tspec(D_MODEL), pl.BlockSpec((None, rows // 2, D_MODEL), lambda b, i: (b, i, 0))),
        scratch_shapes=[pltpu.VMEM((rows, D_MODEL), F32)],
        compiler_params=_params("parallel", "parallel"),
        name="prompt_mix",
    )(sinks, x, q, k2, k2, v2, v2, cu, cu, gc, sa, g1, sc2, sh2, cw, cb, wout, n2w)


def _sample_attn_kernel(sinks_ref, q_ref, ck_ref, kn_ref, cv_ref, vn_ref, o_ref):
    pad = jnp.zeros((WINDOW - SAMPLE_Q_ROWS, 2 * KV_W), BF16)
    blocks = []
    for n in range(q_ref.shape[0]):
        kext = jnp.concatenate([ck_ref[n], kn_ref[n], pad], axis=0)
        vext = jnp.concatenate([cv_ref[n], vn_ref[n], pad], axis=0)
        blocks.append((q_ref[n], kext, vext, 0))
    for n, o in enumerate(_attend_blocks(blocks, sinks_ref)):
        o_ref[n] = o


def _sample_attn(sinks, q8, ck2, k2n, cv2, v2n):
    nseq = q8.shape[0]
    sb = SAMPLE_SEQ_BLOCK
    spec = lambda r, w: pl.BlockSpec((sb, r, w), lambda n: (n, 0, 0))
    return pl.pallas_call(
        _sample_attn_kernel,
        out_shape=jax.ShapeDtypeStruct((nseq, SAMPLE_Q_ROWS, D_MODEL), F32),
        grid=(nseq // sb,),
        in_specs=[
            pl.BlockSpec(memory_space=pltpu.SMEM),
            spec(SAMPLE_Q_ROWS, D_MODEL),
            spec(WINDOW, 2 * KV_W), spec(SAMPLE_Q_ROWS, 2 * KV_W),
            spec(WINDOW, 2 * KV_W), spec(SAMPLE_Q_ROWS, 2 * KV_W),
        ],
        out_specs=spec(SAMPLE_Q_ROWS, D_MODEL),
        compiler_params=_params("parallel"),
        name="sample_attn",
    )(sinks, q8, ck2, k2n, cv2, v2n)


def _sample_mix_kernel(x_ref, cu_ref, p1_ref, p2_ref, gc_ref, sa_ref, oat_ref,
                       g1_ref, sc2_ref, sh2_ref, cw_ref, cb_ref, wout_ref, n2w_ref,
                       x1_ref, h2_ref):
    x1, h2 = _merge_tail(x_ref[...], cu_ref[...], p1_ref[...], p2_ref[...], gc_ref[...],
                         sa_ref[...], oat_ref[...], g1_ref[...], sc2_ref[...], sh2_ref[...],
                         cw_ref, cb_ref[...], wout_ref[...], n2w_ref[...])
    x1_ref[...] = x1
    h2_ref[...] = h2


def _sample_mix(x, cu, prev1, prev2, gc, sa, oattn, g1, sc2, sh2, cw, cb, wout, n2w):
    t = x.shape[0]
    full = lambda: pl.BlockSpec((t, D_MODEL), lambda: (0, 0))
    return pl.pallas_call(
        _sample_mix_kernel,
        out_shape=(jax.ShapeDtypeStruct((t, D_MODEL), F32),
                   jax.ShapeDtypeStruct((t // 2, D_MODEL), jnp.uint32)),
        in_specs=[full() for _ in range(10)] + [
            pl.BlockSpec((CONV_W, D_MODEL), lambda: (0, 0)),
            pl.BlockSpec((1, D_MODEL), lambda: (0, 0)),
            pl.BlockSpec((D_MODEL, D_MODEL), lambda: (0, 0)),
            pl.BlockSpec((1, D_MODEL), lambda: (0, 0)),
        ],
        out_specs=(full(), pl.BlockSpec((t // 2, D_MODEL), lambda: (0, 0))),
        compiler_params=pltpu.CompilerParams(vmem_limit_bytes=VMEM_LIMIT),
        name="sample_mix",
    )(x, cu, prev1, prev2, gc, sa, oattn, g1, sc2, sh2, cw, cb, wout, n2w)


def _top16(s, index):
    rank = jnp.full(s.shape, float(PEER_TOPK), F32)
    vals = []
    for r in range(PEER_TOPK):
        m = jnp.max(s, axis=0, keepdims=True)
        first = jnp.min(jnp.where(s == m, index, float(s.shape[0])), axis=0, keepdims=True)
        hit = index == first
        rank = jnp.where(hit, float(r), rank)
        s = jnp.where(hit, -jnp.inf, s)
        vals.append(m)
    return vals, rank


def _row_array(rows, index):
    out = jnp.zeros(index.shape, F32)
    for r, v in enumerate(rows):
        out = jnp.where(index == float(r), v, out)
    return out


def _gates(s1, s2, v1_0, v2_0, z):
    return jnp.exp(s1 - v1_0) / z, jnp.exp(s2 - v2_0)


def _route_exact(s1, s2, idx16, idx128, idx256):
    v1, rank1 = _top16(s1, idx128)
    v2, rank2 = _top16(s2, idx128)
    v2a = _row_array(v2, idx16)
    cand = jnp.concatenate([v1[r] + v2a for r in range(PEER_TOPK)], axis=0)
    sel = jnp.zeros(cand.shape, F32)
    c = cand
    for _ in range(PEER_TOPK):
        m = jnp.max(c, axis=0, keepdims=True)
        first = jnp.min(jnp.where(c == m, idx256, float(PEER_TOPK * PEER_TOPK)),
                        axis=0, keepdims=True)
        hit = idx256 == first
        sel = jnp.where(hit, 1.0, sel)
        c = jnp.where(hit, -jnp.inf, c)
    z = jnp.sum(sel * jnp.exp(cand - (v1[0] + v2[0])), axis=0, keepdims=True)
    n1 = jnp.zeros(rank1.shape, F32)
    for r in range(PEER_TOPK):
        n_r = jnp.sum(sel[r * PEER_TOPK:(r + 1) * PEER_TOPK], axis=0, keepdims=True)
        n1 = jnp.where(rank1 == float(r), n_r, n1)
    e1, e2 = _gates(s1, s2, v1[0], v2[0], z)
    return n1, e1, rank2, e2


SUBLANES = 8


def _batcher_pairs(n):
    pairs = []
    p = 1
    while p < n:
        k = p
        while k >= 1:
            for j in range(k % p, n - k, 2 * k):
                for i in range(min(k, n - j - k)):
                    if (i + j) // (2 * p) == (i + j + k) // (2 * p):
                        pairs.append((i + j, i + j + k))
            k //= 2
        p *= 2
    return pairs


def _sort_desc(vs):
    vs = list(vs)
    for i, j in _batcher_pairs(len(vs)):
        vs[i], vs[j] = jnp.maximum(vs[i], vs[j]), jnp.minimum(vs[i], vs[j])
    return vs


def _merge_top(a, b):
    n = len(a)
    c = [jnp.maximum(a[i], b[n - 1 - i]) for i in range(n)]
    d = n // 2
    while d >= 1:
        for i in range(n):
            if i & d == 0:
                c[i], c[i + d] = jnp.maximum(c[i], c[i + d]), jnp.minimum(c[i], c[i + d])
        d //= 2
    return c


def _merge_sublanes(w):
    for shift in (4, 2, 1):
        w = _merge_top(w, [pltpu.roll(x, shift, axis=0) for x in w])
    return w


def _tiles(s):
    return [s[SUBLANES * j:SUBLANES * (j + 1)] for j in range(s.shape[0] // SUBLANES)]


def _has_adjacent_tie(v):
    gap = v[0] - v[1]
    for r in range(1, len(v) - 1):
        gap = jnp.minimum(gap, v[r] - v[r + 1])
    return gap == 0.0


def _count_rows(flags):
    acc = flags[0]
    for f in flags[1:]:
        acc = acc + f
    return jnp.sum(acc, axis=0, keepdims=True)


def _route_distinct(s1, s2, sub):
    k = PEER_TOPK
    t1, t2 = _tiles(s1), _tiles(s2)
    v1 = _merge_sublanes(_sort_desc(t1))
    v2 = _merge_sublanes(_sort_desc(t2))

    pick = lambda v: jnp.where(sub < 2.0, v[0], jnp.where(sub < 4.0, v[1], jnp.where(sub < 6.0, v[2], v[3])))
    a_const, b_const = pick(v1), pick(v2)
    lengths = (16, 15, 7, 6, 3, 2, 1)
    lists = []
    for i in range(k):
        a, b = a_const, b_const
        for s_col, s_row, off in ((5.0, 4.0, 2), (3.0, 2.0, 1), (1.0, 0.0, 0)):
            if i < lengths[int(s_col)]:
                a = jnp.where(sub == s_col, v1[i + off + 1], a)
            if i < lengths[int(s_row)]:
                b = jnp.where(sub == s_row, v2[i + off], b)
        live = float(sum(1 for n in lengths if n > i))
        lists.append(jnp.where(sub < live, a + b, -jnp.inf))
    c = _merge_sublanes(lists)
    tau = c[k - 1]

    z = None
    for x in lists:
        e = jnp.where(x >= tau, jnp.exp(x - c[0]), 0.0)
        z = e if z is None else z + e
    z = jnp.sum(z, axis=0, keepdims=True)

    counts, total = [], None
    for r1 in range(k):
        n_r = jnp.zeros_like(tau)
        for r2 in range(k // (r1 + 1)):
            n_r = jnp.where(v1[r1] + v2[r2] >= tau, float(r2 + 1), n_r)
        counts.append(n_r)
        total = n_r if total is None else total + n_r

    n1, rank2, lead1, lead2 = [], [], [], []
    for x1, x2 in zip(t1, t2):
        n = jnp.zeros_like(x1)
        for r in range(k - 1, -1, -1):
            n = jnp.where(x1 >= v1[r], counts[r], n)
        rk = jnp.zeros_like(x2)
        for r in range(k):
            rk = jnp.where(v2[r] > x2, float(r + 1), rk)
        n1.append(n)
        rank2.append(rk)
        lead1.append(jnp.where(x1 >= v1[k - 1], 1.0, 0.0))
        lead2.append(jnp.where(x2 >= v2[k - 1], 1.0, 0.0))
    n1 = jnp.concatenate(n1, axis=0)
    rank2 = jnp.concatenate(rank2, axis=0)

    want = float(k)
    tie = _has_adjacent_tie(v1) | _has_adjacent_tie(v2) | _has_adjacent_tie(c)
    valid = (~tie) & (_count_rows(lead1) == want) & (_count_rows(lead2) == want) & (total == want)
    e1, e2 = _gates(s1, s2, v1[0][0:1], v2[0][0:1], z)
    return n1, e1, rank2, e2, valid


def _route_kernel(h2_ref, wqt_ref, keys_ref, n1_ref, e1_ref, r2_ref, e2_ref, qpt_scr):
    rows = 2 * h2_ref.shape[0]
    qpt_scr[...] = _dot_nt(wqt_ref[...], _unpack_words(h2_ref[...]))
    idx8 = lax.broadcasted_iota(jnp.int32, (8, LANES), 0).astype(F32)
    idx16 = lax.broadcasted_iota(jnp.int32, (PEER_TOPK, LANES), 0).astype(F32)

    groups = PEER_HEADS // ROUTE_HEADS

    def body(it, carry):
        tc = it // groups
        ls = pl.ds(pl.multiple_of(tc * LANES, LANES), LANES)
        hp = lax.Precision.HIGHEST
        heads, scores = [], []
        for j in range(ROUTE_HEADS):
            h = (it % groups) * ROUTE_HEADS + j
            base = pl.multiple_of(h * (2 * PEER_HALF), 2 * PEER_HALF)
            q1 = qpt_scr[pl.ds(base, PEER_HALF), ls]
            q2 = qpt_scr[pl.ds(base + PEER_HALF, PEER_HALF), ls]
            s1 = jnp.dot(keys_ref[2 * h], q1, precision=hp, preferred_element_type=F32)
            s2 = jnp.dot(keys_ref[2 * h + 1], q2, precision=hp, preferred_element_type=F32)
            heads.append(h)
            scores.append((s1, s2))

        def store(h, n1, e1, rank2, e2):
            n1_ref[h, :, ls] = n1
            e1_ref[h, :, ls] = e1
            r2_ref[h, :, ls] = _pack_words(rank2)
            e2_ref[h, :, ls] = _pack_words(e2)

        ties = jnp.zeros((1, LANES), F32)
        for h, (s1, s2) in zip(heads, scores):
            n1, e1, rank2, e2, valid = _route_distinct(s1, s2, idx8)
            store(h, n1, e1, rank2, e2)
            ties = jnp.where(valid, ties, 1.0)

        @pl.when(jnp.max(ties) > 0.0)
        def _():
            idx128 = lax.broadcasted_iota(jnp.int32, (PEER_KEYS, LANES), 0).astype(F32)
            idx256 = lax.broadcasted_iota(jnp.int32, (PEER_TOPK * PEER_TOPK, LANES), 0).astype(F32)
            for h, (s1, s2) in zip(heads, scores):
                store(h, *_route_exact(s1, s2, idx16, idx128, idx256))

        return carry

    lax.fori_loop(0, (rows // LANES) * groups, body, 0)


def _route(h2, wqt, keys):
    t = 2 * h2.shape[0]
    rows = min(ROUTE_ROWS, t)
    out = jax.ShapeDtypeStruct((PEER_HEADS, PEER_KEYS, t), F32)
    outw = jax.ShapeDtypeStruct((PEER_HEADS, PEER_KEYS // 2, t), jnp.uint32)
    ospec = pl.BlockSpec((PEER_HEADS, PEER_KEYS, rows), lambda i: (0, 0, i))
    wspec = pl.BlockSpec((PEER_HEADS, PEER_KEYS // 2, rows), lambda i: (0, 0, i))
    return pl.pallas_call(
        _route_kernel,
        out_shape=(out, out, outw, outw),
        grid=(t // rows,),
        in_specs=[
            pl.BlockSpec((rows // 2, D_MODEL), lambda i: (i, 0)),
            _const_spec((D_MODEL, D_MODEL)),
            _const_spec((2 * PEER_HEADS, PEER_KEYS, PEER_HALF)),
        ],
        out_specs=(ospec, ospec, wspec, wspec),
        scratch_shapes=[pltpu.VMEM((D_MODEL, rows), F32)],
        compiler_params=_params("parallel"),
        name="peer_route",
    )(h2, wqt, keys)


PACK = 16


def _expert_weights(n1_ref, e1_ref, r2_ref, e2_ref, at_ref, hw_ref, tc, il):
    ls = slice(tc * LANES, (tc + 1) * LANES)
    gates = [None] * (PEER_KEYS // PACK)
    for h in range(PEER_HEADS):
        nb = jnp.broadcast_to(n1_ref[h, il:il + 1, ls], (PACK, LANES)).astype(BF16)
        eb = jnp.broadcast_to(e1_ref[h, il:il + 1, ls], (PACK, LANES)).astype(BF16)
        for k in range(PEER_KEYS // PACK):
            ws = slice(PACK // 2 * k, PACK // 2 * (k + 1))
            e2 = _unpack_words(e2_ref[h, ws, ls])
            term = jnp.where(_unpack_words(r2_ref[h, ws, ls]) < nb, e2, jnp.zeros_like(e2)) * eb
            gates[k] = term if h == 0 else gates[k] + term
    for k in range(PEER_KEYS // PACK):
        e0 = il * PEER_KEYS + PACK * k
        a = at_ref[e0:e0 + PACK, ls]
        act = 0.5 * a * (1.0 + lax.erf(a * (2.0 ** -0.5)))
        hw_ref[e0 // 2:(e0 + PACK) // 2, ls] = pltpu.bitcast(gates[k] * act.astype(BF16), jnp.uint32)


MXU_COLS = 256


def _peer_kernel(h2_ref, u_ref, vt_ref,
                 n1p_ref, e1p_ref, r2p_ref, e2p_ref, n1c_ref, e1c_ref, r2c_ref, e2c_ref,
                 x1_ref, g2_ref, y_ref, at0_scr, at1_scr, hw0_scr, hw1_scr, acc_scr,
                 *, pairs_per_block):
    g = pl.program_id(0)
    ec = PEER_CHUNK
    rows = x1_ref.shape[0]

    @pl.when(g == 0)
    def _():
        at1_scr[...] = jnp.zeros_like(at1_scr)
        hw0_scr[...] = jnp.zeros_like(hw0_scr)
        hw1_scr[...] = jnp.zeros_like(hw1_scr)
        acc_scr[...] = jnp.zeros_like(acc_scr)

    pair = jnp.maximum(g - 1, 0) % pairs_per_block

    def half_step(uw_row0, routing, at_w, at_r, hw_w, hw_r, vtw_col0, restart):
        for n in range(rows // MXU_COLS):
            ts = slice(n * MXU_COLS, (n + 1) * MXU_COLS)
            tw = slice(n * MXU_COLS // 2, (n + 1) * MXU_COLS // 2)
            at_w[:, ts] = _dot_nt(_unpack_words(u_ref[uw_row0:uw_row0 + ec // 2, :]),
                                  _unpack_words(h2_ref[tw, :]))
            for tc in range(n * (MXU_COLS // LANES), (n + 1) * (MXU_COLS // LANES)):
                for il in range(ec // PEER_KEYS):
                    _expert_weights(*routing, at_r, hw_w, tc, il)
            acc = acc_scr[:, ts]
            if restart is not None:
                acc = jnp.where(restart, 0.0, acc)
            acc_scr[:, ts] = acc + _dot(_unpack_words(vt_ref[:, vtw_col0:vtw_col0 + ec]),
                                        _unpack_words(hw_r[:, ts]))

    half_step(0, (n1p_ref, e1p_ref, r2p_ref, e2p_ref), at0_scr, at1_scr,
              hw1_scr, hw0_scr, 0, pair == 0)
    half_step(ec // 2, (n1c_ref, e1c_ref, r2c_ref, e2c_ref), at1_scr, at0_scr,
              hw0_scr, hw1_scr, ec, None)

    @pl.when((g > 0) & (pair == pairs_per_block - 1))
    def _():
        y_ref[...] = x1_ref[...] + g2_ref[...] * acc_scr[...].T


def _peer(h2, u, vt, n1, e1, r2, e2, x1, g2):
    t = x1.shape[0]
    rows = min(PEER_ROWS, t)
    n_chunks = PEER_EXPERTS // PEER_CHUNK
    ppb = n_chunks // 2
    n_pairs = (t // rows) * ppb
    n_items = 2 * n_pairs
    il_rows = PEER_CHUNK // PEER_KEYS

    pair_a = lambda g: jnp.minimum(g, n_pairs - 1)
    item_p = lambda g: jnp.maximum(2 * g - 1, 0)
    item_c = lambda g: jnp.minimum(2 * g, n_items - 1)
    block_c = lambda g: jnp.maximum(g - 1, 0) // ppb

    def routing_specs(item):
        per_chunk = pl.BlockSpec((PEER_HEADS, il_rows, rows),
                                 lambda g: (0, item(g) % n_chunks, item(g) // n_chunks))
        per_block = pl.BlockSpec((PEER_HEADS, PEER_KEYS // 2, rows),
                                 lambda g: (0, 0, item(g) // n_chunks))
        return [per_chunk, per_chunk, per_block, per_block]

    if g2.shape[1] == 1:
        blocks_per_seq = (t // g2.shape[0]) // rows
        g2_spec = pl.BlockSpec((None, 1, D_MODEL), lambda g: (block_c(g) // blocks_per_seq, 0, 0))
    else:
        g2_spec = pl.BlockSpec((None, rows, D_MODEL), lambda g: (0, block_c(g), 0))
    tok_c = pl.BlockSpec((rows, D_MODEL), lambda g: (block_c(g), 0))
    return pl.pallas_call(
        functools.partial(_peer_kernel, pairs_per_block=ppb),
        out_shape=jax.ShapeDtypeStruct((t, D_MODEL), F32),
        grid=(n_pairs + 1,),
        in_specs=[
            pl.BlockSpec((rows // 2, D_MODEL), lambda g: (pair_a(g) // ppb, 0)),
            pl.BlockSpec((PEER_CHUNK, D_MODEL), lambda g: (pair_a(g) % ppb, 0)),
            pl.BlockSpec((D_MODEL // 2, 2 * PEER_CHUNK), lambda g: (0, jnp.maximum(g - 1, 0) % ppb)),
            *routing_specs(item_p), *routing_specs(item_c),
            tok_c, g2_spec,
        ],
        out_specs=tok_c,
        scratch_shapes=[pltpu.VMEM((PEER_CHUNK, rows), F32),
                        pltpu.VMEM((PEER_CHUNK, rows), F32),
                        pltpu.VMEM((PEER_CHUNK // 2, rows), jnp.uint32),
                        pltpu.VMEM((PEER_CHUNK // 2, rows), jnp.uint32),
                        pltpu.VMEM((D_MODEL, rows), F32)],
        compiler_params=_params("arbitrary"),
        name="peer_experts",
    )(h2, u, vt, n1, e1, r2, e2, n1, e1, r2, e2, x1, g2)


def _dup_heads(w):
    lead = w.shape[:-1]
    w = w.reshape(lead + (N_KV_HEADS, 1, HEAD_DIM))
    return jnp.broadcast_to(w, lead + (N_KV_HEADS, 2, HEAD_DIM)).reshape(lead + (2 * KV_W,))


def _first_of_dup(a):
    return a.reshape(a.shape[:-1] + (N_KV_HEADS, 2, HEAD_DIM))[..., 0, :]


def kernel(x_prompt, x_sample, cache_k, cache_v, state_conv, c_prompt, c_sample, w_ada, b_ada, norm1_w, w_in, q_norm_w, k_norm_w, sinks, conv_w, conv_bias, w_out, norm2_w, w_query, sub_keys, expert_u, expert_v):
    depth = w_ada.shape[0]
    batch, seq, _ = x_prompt.shape
    nseq, dec = x_sample.shape[:2]
    assert dec <= SAMPLE_Q_ROWS and dec >= CONV_W - 1
    ts = nseq * dec

    head_of_col = jnp.arange(D_MODEL, dtype=jnp.int32) // HEAD_DIM
    hsum = (head_of_col[:, None] == jnp.arange(LANES, dtype=jnp.int32)[None, :]).astype(BF16)
    hexp = hsum.T
    c_all = jnp.concatenate([c_prompt, c_sample], axis=0)
    c_all = jnp.pad(c_all, ((0, (-c_all.shape[0]) % 8), (0, 0)))

    yp = x_prompt
    ys = x_sample.reshape(1, ts, D_MODEL)
    kp, vp, cp, ksm, vsm, csm = [], [], [], [], [], []
    for l in range(depth):
        win = w_in[l]
        win2 = jnp.concatenate(
            [win[:, :D_MODEL], _dup_heads(win[:, D_MODEL:D_MODEL + KV_W]),
             _dup_heads(win[:, D_MODEL + KV_W:D_MODEL + 2 * KV_W]), win[:, D_MODEL + 2 * KV_W:]],
            axis=1).astype(BF16)
        nw1 = norm1_w[l].reshape(1, D_MODEL)
        nw2 = norm2_w[l].reshape(1, D_MODEL)
        qw = jnp.tile(q_norm_w[l], N_HEADS).reshape(1, D_MODEL)
        kw = jnp.tile(k_norm_w[l], 2 * N_KV_HEADS).reshape(1, 2 * KV_W)
        cw = conv_w[l]
        cb = conv_bias[l].reshape(1, D_MODEL)
        wout = w_out[l].astype(BF16)
        wqt = w_query[l].T.astype(BF16)
        keys = sub_keys[l].reshape(2 * PEER_HEADS, PEER_KEYS, PEER_HALF)
        u = _pack_table(expert_u[l])
        vt = _pack_table_transposed(expert_v[l])

        mod = _adaln(c_all, w_ada[l], b_ada[l])
        mod_p = [mod[j, :batch].reshape(batch, 1, D_MODEL) for j in range(6)]
        mod_s = [jnp.repeat(mod[j, batch:batch + nseq], dec, axis=0).reshape(1, ts, D_MODEL)
                 for j in range(6)]

        q, k2, v2, cu, gc, sa = _project(yp, mod_p[1], mod_p[0], nw1, win2, qw, kw, hsum, hexp)
        x1, h2 = _prompt_mix(sinks[l], yp, q, k2, v2, cu, gc, sa, mod_p[2], mod_p[4], mod_p[3],
                             cw, cb, wout, nw2)
        h2 = h2.reshape(batch * seq // 2, D_MODEL)
        n1, e1, r2, e2 = _route(h2, wqt, keys)
        yp = _peer(h2, u, vt, n1, e1, r2, e2, x1.reshape(batch * seq, D_MODEL), mod_p[5])
        yp = yp.reshape(batch, seq, D_MODEL)
        kp.append(_first_of_dup(k2[:, seq - WINDOW:]))
        vp.append(_first_of_dup(v2[:, seq - WINDOW:]))
        cp.append(cu[:, seq - (CONV_W - 1):])

        q, k2, v2, cu, gc, sa = _project(ys, mod_s[1], mod_s[0], nw1, win2, qw, kw, hsum, hexp)
        pad_rows = lambda a: jnp.pad(a.reshape(nseq, dec, a.shape[-1]),
                                     ((0, 0), (0, SAMPLE_Q_ROWS - dec), (0, 0)))
        oat = _sample_attn(sinks[l], pad_rows(q[0]),
                           _dup_heads(cache_k[l].reshape(nseq, WINDOW, KV_W)).astype(BF16),
                           pad_rows(k2[0]).astype(BF16),
                           _dup_heads(cache_v[l].reshape(nseq, WINDOW, KV_W)).astype(BF16),
                           pad_rows(v2[0]).astype(BF16))
        oat = oat[:, :dec].reshape(ts, D_MODEL)
        cu_seq = cu.reshape(nseq, dec, D_MODEL)
        up = jnp.concatenate([state_conv[l], cu_seq], axis=1)
        prev2 = up[:, 0:dec].reshape(ts, D_MODEL)
        prev1 = up[:, 1:dec + 1].reshape(ts, D_MODEL)
        flat = lambda a: a.reshape(ts, D_MODEL)
        x1, h2 = _sample_mix(flat(ys), flat(cu), prev1, prev2, flat(gc), flat(sa), oat,
                             flat(mod_s[2]), flat(mod_s[4]), flat(mod_s[3]), cw, cb, wout, nw2)
        n1, e1, r2, e2 = _route(h2, wqt, keys)
        ys = _peer(h2, u, vt, n1, e1, r2, e2, x1, mod_s[5]).reshape(1, ts, D_MODEL)
        k_new = _first_of_dup(k2[0]).reshape(nseq, dec, N_KV_HEADS, HEAD_DIM)
        v_new = _first_of_dup(v2[0]).reshape(nseq, dec, N_KV_HEADS, HEAD_DIM)
        ksm.append(jnp.concatenate([cache_k[l], k_new], axis=1)[:, -WINDOW:])
        vsm.append(jnp.concatenate([cache_v[l], v_new], axis=1)[:, -WINDOW:])
        csm.append(up[:, -(CONV_W - 1):])

    return (yp, ys.reshape(nseq, dec, D_MODEL), jnp.stack(kp), jnp.stack(vp), jnp.stack(cp),
            jnp.stack(ksm), jnp.stack(vsm), jnp.stack(csm))
```

```python
import functools

import jax
import jax.numpy as jnp
from jax import lax
from jax.experimental import pallas as pl
from jax.experimental.pallas import tpu as pltpu

F32 = jnp.float32
BF16 = jnp.bfloat16

D_MODEL = 1024
N_HEADS = 16
N_KV_HEADS = 4
HEAD_DIM = 64
WINDOW = 128
KV_W = N_KV_HEADS * HEAD_DIM
CONV_W = 3
PEER_HEADS = 8
PEER_KEYS = 128
PEER_TOPK = 16
PEER_HALF = 64
PEER_EXPERTS = PEER_KEYS * PEER_KEYS
RMS_EPS = 1e-6

LANES = 128
VMEM_LIMIT = 56 * 1024 * 1024

PROJ_ROWS = 512
ATTN_ROWS = 512
PACK_ROWS = 1024
ROUTE_ROWS = 512
ROUTE_HEADS = 4
PEER_ROWS = 512
PEER_CHUNK = 1024
SAMPLE_Q_ROWS = 8
SAMPLE_SEQ_BLOCK = 8

C_Q = 0
C_K = C_Q + D_MODEL
C_V = C_K + 2 * KV_W
C_U = C_V + 2 * KV_W
C_GPOST = C_U + D_MODEL
C_GPRE = C_GPOST + D_MODEL
C_GATTN = C_GPRE + D_MODEL
C_GCONV = C_GATTN + D_MODEL
C_END = C_GCONV + D_MODEL


def _params(*semantics):
    return pltpu.CompilerParams(dimension_semantics=semantics, vmem_limit_bytes=VMEM_LIMIT)


def _const_spec(shape):
    nd = len(shape)
    return pl.BlockSpec(shape, lambda *_: (0,) * nd, pipeline_mode=pl.Buffered(1))


def _mod_spec(arr, rows):
    if arr.shape[1] == 1:
        return pl.BlockSpec((None, 1, D_MODEL), lambda b, i, *_: (b, 0, 0))
    return pl.BlockSpec((None, rows, D_MODEL), lambda b, i, *_: (b, i, 0))


def _dot(a, b):
    return jnp.dot(a, b, preferred_element_type=F32)


def _dot_nt(a, b):
    return lax.dot_general(a, b, (((1,), (1,)), ((), ())), preferred_element_type=F32)


def _pack_words(x):
    return pltpu.bitcast(x.astype(BF16), jnp.uint32)


def _unpack_words(w):
    return pltpu.bitcast(w, BF16)


def _dot_split(a, b):
    hi = a.astype(BF16)
    lo = (a - hi.astype(F32)).astype(BF16)
    return _dot(hi, b) + _dot(lo, b)


def _ada_kernel(c_ref, w_ref, b_ref, o_ref):
    s = jax.nn.silu(c_ref[...])
    o_ref[...] = jnp.dot(s, w_ref[...], precision=lax.Precision.HIGHEST,
                         preferred_element_type=F32) + b_ref[...]


def _adaln(c_all, w_ada, b_ada):
    rows = c_all.shape[0]
    return pl.pallas_call(
        _ada_kernel,
        out_shape=jax.ShapeDtypeStruct((6, rows, D_MODEL), F32),
        grid=(6,),
        in_specs=[
            pl.BlockSpec((rows, D_MODEL), lambda j: (0, 0)),
            pl.BlockSpec((D_MODEL, D_MODEL), lambda j: (0, j)),
            pl.BlockSpec((1, D_MODEL), lambda j: (0, j)),
        ],
        out_specs=pl.BlockSpec((None, rows, D_MODEL), lambda j: (j, 0, 0)),
        compiler_params=_params("arbitrary"),
        name="adaln",
    )(c_all, w_ada, b_ada.reshape(1, 6 * D_MODEL))


def _pack_kernel(x_ref, o_ref):
    o_ref[...] = _pack_words(x_ref[...])


def _pack_t_kernel(x_ref, o_ref):
    o_ref[...] = _pack_words(x_ref[...].T)


def _pack_table(x):
    n, m = x.shape
    return pl.pallas_call(
        _pack_kernel,
        out_shape=jax.ShapeDtypeStruct((n // 2, m), jnp.uint32),
        grid=(n // PACK_ROWS,),
        in_specs=[pl.BlockSpec((PACK_ROWS, m), lambda i: (i, 0))],
        out_specs=pl.BlockSpec((PACK_ROWS // 2, m), lambda i: (i, 0)),
        compiler_params=_params("parallel"),
        name="pack_table",
    )(x)


def _pack_table_transposed(x):
    n, m = x.shape
    return pl.pallas_call(
        _pack_t_kernel,
        out_shape=jax.ShapeDtypeStruct((m // 2, n), jnp.uint32),
        grid=(n // PACK_ROWS,),
        in_specs=[pl.BlockSpec((PACK_ROWS, m), lambda i: (i, 0))],
        out_specs=pl.BlockSpec((m // 2, PACK_ROWS), lambda i: (0, i)),
        compiler_params=_params("parallel"),
        name="pack_table_t",
    )(x)


def _head_rms(z, hsum, hexp, w):
    ss = _dot_split(z * z, hsum)
    r = lax.rsqrt(ss * (1.0 / HEAD_DIM) + RMS_EPS)
    return z * _dot_split(r, hexp) * w


def _proj_kernel(x_ref, sc_ref, sh_ref, nw_ref, win_ref, qw_ref, kw_ref, hsum_ref, hexp_ref,
                 q_ref, k2_ref, v2_ref, cu_ref, gc_ref, sa_ref):
    x = x_ref[...]
    h = x * lax.rsqrt(jnp.mean(x * x, axis=-1, keepdims=True) + RMS_EPS) * nw_ref[...]
    h = h * (1.0 + sc_ref[...]) + sh_ref[...]
    hb = h.astype(BF16)

    def seg(lo, hi):
        return _dot(hb, win_ref[:, lo:hi])

    q = _head_rms(seg(C_Q, C_K), hsum_ref[...], hexp_ref[...], qw_ref[...])
    q_ref[...] = (q * (HEAD_DIM ** -0.5)).astype(BF16)
    nk = 2 * KV_W
    k2_ref[...] = _head_rms(seg(C_K, C_V), hsum_ref[:nk, :], hexp_ref[:, :nk], kw_ref[...])
    v2_ref[...] = seg(C_V, C_U)
    cu_ref[...] = seg(C_GPRE, C_GATTN) * seg(C_U, C_GPOST)
    gc_ref[...] = (seg(C_GPOST, C_GPRE) * jax.nn.sigmoid(seg(C_GCONV, C_END))).astype(BF16)
    sa_ref[...] = jax.nn.sigmoid(seg(C_GATTN, C_GCONV)).astype(BF16)


def _project(x, sc1, sh1, nw, win2, qw, kw, hsum, hexp):
    nb, s, _ = x.shape
    rows = min(PROJ_ROWS, s)
    tok = lambda w, dt: jax.ShapeDtypeStruct((nb, s, w), dt)
    tspec = lambda w: pl.BlockSpec((None, rows, w), lambda b, i: (b, i, 0))
    return pl.pallas_call(
        _proj_kernel,
        out_shape=(tok(D_MODEL, BF16), tok(2 * KV_W, F32), tok(2 * KV_W, F32),
                   tok(D_MODEL, F32), tok(D_MODEL, BF16), tok(D_MODEL, BF16)),
        grid=(nb, s // rows),
        in_specs=[
            tspec(D_MODEL), _mod_spec(sc1, rows), _mod_spec(sh1, rows),
            _const_spec((1, D_MODEL)), _const_spec((D_MODEL, C_END)),
            _const_spec((1, D_MODEL)), _const_spec((1, 2 * KV_W)),
            _const_spec((D_MODEL, LANES)), _const_spec((LANES, D_MODEL)),
        ],
        out_specs=(tspec(D_MODEL), tspec(2 * KV_W), tspec(2 * KV_W),
                   tspec(D_MODEL), tspec(D_MODEL), tspec(D_MODEL)),
        compiler_params=_params("parallel", "parallel"),
        name="in_proj",
    )(x, sc1, sh1, nw, win2, qw, kw, hsum, hexp)


def _attend_blocks(blocks, sinks_ref):
    qb = blocks[0][0].shape[0]
    lane = lax.broadcasted_iota(jnp.int32, (qb, LANES), 1)
    low = lane < HEAD_DIM
    row = lax.broadcasted_iota(jnp.int32, (4 * qb, 2 * WINDOW), 0)
    col = lax.broadcasted_iota(jnp.int32, (4 * qb, 2 * WINDOW), 1)
    dist = col - (row & (qb - 1))
    band = (dist >= 1) & (dist <= WINDOW)
    blk = lax.broadcasted_iota(jnp.int32, (4 * qb, 1), 0) // qb
    ones = jnp.ones((2 * WINDOW, LANES), BF16)
    sinks = []
    for j in range(N_KV_HEADS):
        sink = jnp.full((4 * qb, 1), sinks_ref[4 * j + 3], F32)
        for g in range(2, -1, -1):
            sink = jnp.where(blk == g, sinks_ref[4 * j + g], sink)
        sinks.append(sink)

    scores = []
    for qblk, k2, _, _ in blocks:
        for j in range(N_KV_HEADS):
            qa = qblk[:, 2 * j * LANES:(2 * j + 1) * LANES].astype(F32)
            qc = qblk[:, (2 * j + 1) * LANES:(2 * j + 2) * LANES].astype(F32)
            lhs = jnp.concatenate([jnp.where(low, qa, 0.0), jnp.where(low, 0.0, qa),
                                   jnp.where(low, qc, 0.0), jnp.where(low, 0.0, qc)], axis=0)
            scores.append(_dot_nt(lhs.astype(BF16), k2[:, j * LANES:(j + 1) * LANES]))

    probs = []
    for b, (_, _, _, first_col) in enumerate(blocks):
        mask = band if isinstance(first_col, int) and first_col == 0 else band & (col >= first_col)
        for j in range(N_KV_HEADS):
            s = jnp.where(mask, scores[b * N_KV_HEADS + j], -jnp.inf)
            m = jnp.maximum(jnp.max(s, axis=-1, keepdims=True), sinks[j])
            probs.append((jnp.exp(s - m).astype(BF16), jnp.exp(sinks[j] - m)))

    results = []
    for b, (_, _, v2, _) in enumerate(blocks):
        outs = []
        for j in range(N_KV_HEADS):
            p, sink_term = probs[b * N_KV_HEADS + j]
            o = _dot(p, jnp.concatenate([v2[:, j * LANES:(j + 1) * LANES], ones], axis=1))
            on = o[:, :LANES] / (o[:, LANES:] + sink_term)
            outs.append(jnp.where(low, on[0:qb], on[qb:2 * qb]))
            outs.append(jnp.where(low, on[2 * qb:3 * qb], on[3 * qb:4 * qb]))
        results.append(jnp.concatenate(outs, axis=1))
    return results


def _merge_tail(x, cu, prev1, prev2, gc, sa, oattn, g1, sc2, sh2, cw_ref, cb, wout, n2w):
    conv_y = prev2 * cw_ref[0:1, :] + prev1 * cw_ref[1:2, :] + cu * cw_ref[2:3, :] + cb
    merged = sa.astype(F32) * oattn + gc.astype(F32) * conv_y
    x1 = x + g1 * _dot(merged.astype(BF16), wout)
    h2 = x1 * lax.rsqrt(jnp.mean(x1 * x1, axis=-1, keepdims=True) + RMS_EPS) * n2w
    h2 = h2 * (1.0 + sc2) + sh2
    return x1, _pack_words(h2)


def _prompt_mix_kernel(sinks_ref, x_ref, q_ref, k_ref, kp_ref, v_ref, vp_ref, cu_ref, cup_ref,
                       gc_ref, sa_ref, g1_ref, sc2_ref, sh2_ref, cw_ref, cb_ref, wout_ref, n2w_ref,
                       x1_ref, h2_ref, oat_scr):
    i = pl.program_id(1)
    rows = x_ref.shape[0]
    kext = jnp.concatenate([kp_ref[...], k_ref[...]], axis=0).astype(BF16)
    vext = jnp.concatenate([vp_ref[...], v_ref[...]], axis=0).astype(BF16)
    for w in range(rows // WINDOW):
        first_col = jnp.where(i == 0, WINDOW, 0) if w == 0 else 0
        lo = w * WINDOW
        oat_scr[lo:lo + WINDOW, :] = _attend_blocks(
            [(q_ref[lo:lo + WINDOW, :], kext[lo:lo + 2 * WINDOW, :], vext[lo:lo + 2 * WINDOW, :],
              first_col)], sinks_ref)[0]

    cu = cu_ref[...]
    has_prev = (i > 0).astype(F32)
    p1 = cup_ref[7:8, :] * has_prev
    p2 = cup_ref[6:7, :] * has_prev
    r = lax.broadcasted_iota(jnp.int32, (rows, 1), 0)
    prev1 = jnp.where(r == 0, p1, pltpu.roll(cu, 1, axis=0))
    prev2 = jnp.where(r == 0, p2, jnp.where(r == 1, p1, pltpu.roll(cu, 2, axis=0)))
    x1, h2 = _merge_tail(x_ref[...], cu, prev1, prev2, gc_ref[...], sa_ref[...], oat_scr[...],
                         g1_ref[...], sc2_ref[...], sh2_ref[...], cw_ref, cb_ref[...],
                         wout_ref[...], n2w_ref[...])
    x1_ref[...] = x1
    h2_ref[...] = h2


def _prompt_mix(sinks, x, q, k2, v2, cu, gc, sa, g1, sc2, sh2, cw, cb, wout, n2w):
    nb, s, _ = x.shape
    rows = ATTN_ROWS
    wpt = rows // WINDOW
    tspec = lambda w: pl.BlockSpec((None, rows, w), lambda b, i: (b, i, 0))
    prev_win = pl.BlockSpec((None, WINDOW, 2 * KV_W), lambda b, i: (b, jnp.maximum(i * wpt - 1, 0), 0))
    prev_rows = pl.BlockSpec((None, 8, D_MODEL), lambda b, i: (b, jnp.maximum(i * (rows // 8) - 1, 0), 0))
    mod = pl.BlockSpec((None, 1, D_MODEL), lambda b, i: (b, 0, 0))
    return pl.pallas_call(
        _prompt_mix_kernel,
        out_shape=(jax.ShapeDtypeStruct((nb, s, D_MODEL), F32),
                   jax.ShapeDtypeStruct((nb, s // 2, D_MODEL), jnp.uint32)),
        grid=(nb, s // rows),
        in_specs=[
            pl.BlockSpec(memory_space=pltpu.SMEM),
            tspec(D_MODEL), tspec(D_MODEL),
            tspec(2 * KV_W), prev_win, tspec(2 * KV_W), prev_win,
            tspec(D_MODEL), prev_rows, tspec(D_MODEL), tspec(D_MODEL),
            mod, mod, mod,
            _const_spec((CONV_W, D_MODEL)), _const_spec((1, D_MODEL)),
            _const_spec((D_MODEL, D_MODEL)), _const_spec((1, D_MODEL)),
        ],
        out_specs=(tspec(D_MODEL), pl.BlockSpec((None, rows // 2, D_MODEL), lambda b, i: (b, i, 0))),
        scratch_shapes=[pltpu.VMEM((rows, D_MODEL), F32)],
        compiler_params=_params("parallel", "parallel"),
        name="prompt_mix",
    )(sinks, x, q, k2, k2, v2, v2, cu, cu, gc, sa, g1, sc2, sh2, cw, cb, wout, n2w)


def _sample_attn_kernel(sinks_ref, q_ref, ck_ref, kn_ref, cv_ref, vn_ref, o_ref):
    pad = jnp.zeros((WINDOW - SAMPLE_Q_ROWS, 2 * KV_W), BF16)
    blocks = []
    for n in range(q_ref.shape[0]):
        kext = jnp.concatenate([ck_ref[n], kn_ref[n], pad], axis=0)
        vext = jnp.concatenate([cv_ref[n], vn_ref[n], pad], axis=0)
        blocks.append((q_ref[n], kext, vext, 0))
    for n, o in enumerate(_attend_blocks(blocks, sinks_ref)):
        o_ref[n] = o


def _sample_attn(sinks, q8, ck2, k2n, cv2, v2n):
    nseq = q8.shape[0]
    sb = SAMPLE_SEQ_BLOCK
    spec = lambda r, w: pl.BlockSpec((sb, r, w), lambda n: (n, 0, 0))
    return pl.pallas_call(
        _sample_attn_kernel,
        out_shape=jax.ShapeDtypeStruct((nseq, SAMPLE_Q_ROWS, D_MODEL), F32),
        grid=(nseq // sb,),
        in_specs=[
            pl.BlockSpec(memory_space=pltpu.SMEM),
            spec(SAMPLE_Q_ROWS, D_MODEL),
            spec(WINDOW, 2 * KV_W), spec(SAMPLE_Q_ROWS, 2 * KV_W),
            spec(WINDOW, 2 * KV_W), spec(SAMPLE_Q_ROWS, 2 * KV_W),
        ],
        out_specs=spec(SAMPLE_Q_ROWS, D_MODEL),
        compiler_params=_params("parallel"),
        name="sample_attn",
    )(sinks, q8, ck2, k2n, cv2, v2n)


def _sample_mix_kernel(x_ref, cu_ref, p1_ref, p2_ref, gc_ref, sa_ref, oat_ref,
                       g1_ref, sc2_ref, sh2_ref, cw_ref, cb_ref, wout_ref, n2w_ref,
                       x1_ref, h2_ref):
    x1, h2 = _merge_tail(x_ref[...], cu_ref[...], p1_ref[...], p2_ref[...], gc_ref[...],
                         sa_ref[...], oat_ref[...], g1_ref[...], sc2_ref[...], sh2_ref[...],
                         cw_ref, cb_ref[...], wout_ref[...], n2w_ref[...])
    x1_ref[...] = x1
    h2_ref[...] = h2


def _sample_mix(x, cu, prev1, prev2, gc, sa, oattn, g1, sc2, sh2, cw, cb, wout, n2w):
    t = x.shape[0]
    full = lambda: pl.BlockSpec((t, D_MODEL), lambda: (0, 0))
    return pl.pallas_call(
        _sample_mix_kernel,
        out_shape=(jax.ShapeDtypeStruct((t, D_MODEL), F32),
                   jax.ShapeDtypeStruct((t // 2, D_MODEL), jnp.uint32)),
        in_specs=[full() for _ in range(10)] + [
            pl.BlockSpec((CONV_W, D_MODEL), lambda: (0, 0)),
            pl.BlockSpec((1, D_MODEL), lambda: (0, 0)),
            pl.BlockSpec((D_MODEL, D_MODEL), lambda: (0, 0)),
            pl.BlockSpec((1, D_MODEL), lambda: (0, 0)),
        ],
        out_specs=(full(), pl.BlockSpec((t // 2, D_MODEL), lambda: (0, 0))),
        compiler_params=pltpu.CompilerParams(vmem_limit_bytes=VMEM_LIMIT),
        name="sample_mix",
    )(x, cu, prev1, prev2, gc, sa, oattn, g1, sc2, sh2, cw, cb, wout, n2w)


def _top16(s, index):
    rank = jnp.full(s.shape, float(PEER_TOPK), F32)
    vals = []
    for r in range(PEER_TOPK):
        m = jnp.max(s, axis=0, keepdims=True)
        first = jnp.min(jnp.where(s == m, index, float(s.shape[0])), axis=0, keepdims=True)
        hit = index == first
        rank = jnp.where(hit, float(r), rank)
        s = jnp.where(hit, -jnp.inf, s)
        vals.append(m)
    return vals, rank


def _row_array(rows, index):
    out = jnp.zeros(index.shape, F32)
    for r, v in enumerate(rows):
        out = jnp.where(index == float(r), v, out)
    return out


def _gates(s1, s2, v1_0, v2_0, z):
    return jnp.exp(s1 - v1_0) / z, jnp.exp(s2 - v2_0)


def _route_exact(s1, s2, idx16, idx128, idx256):
    v1, rank1 = _top16(s1, idx128)
    v2, rank2 = _top16(s2, idx128)
    v2a = _row_array(v2, idx16)
    cand = jnp.concatenate([v1[r] + v2a for r in range(PEER_TOPK)], axis=0)
    sel = jnp.zeros(cand.shape, F32)
    c = cand
    for _ in range(PEER_TOPK):
        m = jnp.max(c, axis=0, keepdims=True)
        first = jnp.min(jnp.where(c == m, idx256, float(PEER_TOPK * PEER_TOPK)),
                        axis=0, keepdims=True)
        hit = idx256 == first
        sel = jnp.where(hit, 1.0, sel)
        c = jnp.where(hit, -jnp.inf, c)
    z = jnp.sum(sel * jnp.exp(cand - (v1[0] + v2[0])), axis=0, keepdims=True)
    n1 = jnp.zeros(rank1.shape, F32)
    for r in range(PEER_TOPK):
        n_r = jnp.sum(sel[r * PEER_TOPK:(r + 1) * PEER_TOPK], axis=0, keepdims=True)
        n1 = jnp.where(rank1 == float(r), n_r, n1)
    e1, e2 = _gates(s1, s2, v1[0], v2[0], z)
    return n1, e1, rank2, e2


SUBLANES = 8


def _batcher_pairs(n):
    pairs = []
    p = 1
    while p < n:
        k = p
        while k >= 1:
            for j in range(k % p, n - k, 2 * k):
                for i in range(min(k, n - j - k)):
                    if (i + j) // (2 * p) == (i + j + k) // (2 * p):
                        pairs.append((i + j, i + j + k))
            k //= 2
        p *= 2
    return pairs


def _sort_desc(vs):
    vs = list(vs)
    for i, j in _batcher_pairs(len(vs)):
        vs[i], vs[j] = jnp.maximum(vs[i], vs[j]), jnp.minimum(vs[i], vs[j])
    return vs


def _merge_top(a, b):
    n = len(a)
    c = [jnp.maximum(a[i], b[n - 1 - i]) for i in range(n)]
    d = n // 2
    while d >= 1:
        for i in range(n):
            if i & d == 0:
                c[i], c[i + d] = jnp.maximum(c[i], c[i + d]), jnp.minimum(c[i], c[i + d])
        d //= 2
    return c


def _merge_sublanes(w):
    for shift in (4, 2, 1):
        w = _merge_top(w, [pltpu.roll(x, shift, axis=0) for x in w])
    return w


def _tiles(s):
    return [s[SUBLANES * j:SUBLANES * (j + 1)] for j in range(s.shape[0] // SUBLANES)]


def _has_adjacent_tie(v):
    gap = v[0] - v[1]
    for r in range(1, len(v) - 1):
        gap = jnp.minimum(gap, v[r] - v[r + 1])
    return gap == 0.0


def _count_rows(flags):
    acc = flags[0]
    for f in flags[1:]:
        acc = acc + f
    return jnp.sum(acc, axis=0, keepdims=True)


def _route_distinct(s1, s2, sub):
    k = PEER_TOPK
    t1, t2 = _tiles(s1), _tiles(s2)
    v1 = _merge_sublanes(_sort_desc(t1))
    v2 = _merge_sublanes(_sort_desc(t2))

    pick = lambda v: jnp.where(sub < 2.0, v[0], jnp.where(sub < 4.0, v[1], jnp.where(sub < 6.0, v[2], v[3])))
    a_const, b_const = pick(v1), pick(v2)
    lengths = (16, 15, 7, 6, 3, 2, 1)
    lists = []
    for i in range(k):
        a, b = a_const, b_const
        for s_col, s_row, off in ((5.0, 4.0, 2), (3.0, 2.0, 1), (1.0, 0.0, 0)):
            if i < lengths[int(s_col)]:
                a = jnp.where(sub == s_col, v1[i + off + 1], a)
            if i < lengths[int(s_row)]:
                b = jnp.where(sub == s_row, v2[i + off], b)
        live = float(sum(1 for n in lengths if n > i))
        lists.append(jnp.where(sub < live, a + b, -jnp.inf))
    c = _merge_sublanes(lists)
    tau = c[k - 1]

    z = None
    for x in lists:
        e = jnp.where(x >= tau, jnp.exp(x - c[0]), 0.0)
        z = e if z is None else z + e
    z = jnp.sum(z, axis=0, keepdims=True)

    counts, total = [], None
    for r1 in range(k):
        n_r = jnp.zeros_like(tau)
        for r2 in range(k // (r1 + 1)):
            n_r = jnp.where(v1[r1] + v2[r2] >= tau, float(r2 + 1), n_r)
        counts.append(n_r)
        total = n_r if total is None else total + n_r

    n1, rank2, lead1, lead2 = [], [], [], []
    for x1, x2 in zip(t1, t2):
        n = jnp.zeros_like(x1)
        for r in range(k - 1, -1, -1):
            n = jnp.where(x1 >= v1[r], counts[r], n)
        rk = jnp.zeros_like(x2)
        for r in range(k):
            rk = jnp.where(v2[r] > x2, float(r + 1), rk)
        n1.append(n)
        rank2.append(rk)
        lead1.append(jnp.where(x1 >= v1[k - 1], 1.0, 0.0))
        lead2.append(jnp.where(x2 >= v2[k - 1], 1.0, 0.0))
    n1 = jnp.concatenate(n1, axis=0)
    rank2 = jnp.concatenate(rank2, axis=0)

    want = float(k)
    tie = _has_adjacent_tie(v1) | _has_adjacent_tie(v2) | _has_adjacent_tie(c)
    valid = (~tie) & (_count_rows(lead1) == want) & (_count_rows(lead2) == want) & (total == want)
    e1, e2 = _gates(s1, s2, v1[0][0:1], v2[0][0:1], z)
    return n1, e1, rank2, e2, valid


def _route_kernel(h2_ref, wqt_ref, keys_ref, n1_ref, e1_ref, r2_ref, e2_ref, qpt_scr):
    rows = 2 * h2_ref.shape[0]
    qpt_scr[...] = _dot_nt(wqt_ref[...], _unpack_words(h2_ref[...]))
    idx8 = lax.broadcasted_iota(jnp.int32, (8, LANES), 0).astype(F32)
    idx16 = lax.broadcasted_iota(jnp.int32, (PEER_TOPK, LANES), 0).astype(F32)

    groups = PEER_HEADS // ROUTE_HEADS

    def body(it, carry):
        tc = it // groups
        ls = pl.ds(pl.multiple_of(tc * LANES, LANES), LANES)
        hp = lax.Precision.HIGHEST
        heads, scores = [], []
        for j in range(ROUTE_HEADS):
            h = (it % groups) * ROUTE_HEADS + j
            base = pl.multiple_of(h * (2 * PEER_HALF), 2 * PEER_HALF)
            q1 = qpt_scr[pl.ds(base, PEER_HALF), ls]
            q2 = qpt_scr[pl.ds(base + PEER_HALF, PEER_HALF), ls]
            s1 = jnp.dot(keys_ref[2 * h], q1, precision=hp, preferred_element_type=F32)
            s2 = jnp.dot(keys_ref[2 * h + 1], q2, precision=hp, preferred_element_type=F32)
            heads.append(h)
            scores.append((s1, s2))

        def store(h, n1, e1, rank2, e2):
            n1_ref[h, :, ls] = n1
            e1_ref[h, :, ls] = e1
            r2_ref[h, :, ls] = _pack_words(rank2)
            e2_ref[h, :, ls] = _pack_words(e2)

        ties = jnp.zeros((1, LANES), F32)
        for h, (s1, s2) in zip(heads, scores):
            n1, e1, rank2, e2, valid = _route_distinct(s1, s2, idx8)
            store(h, n1, e1, rank2, e2)
            ties = jnp.where(valid, ties, 1.0)

        @pl.when(jnp.max(ties) > 0.0)
        def _():
            idx128 = lax.broadcasted_iota(jnp.int32, (PEER_KEYS, LANES), 0).astype(F32)
            idx256 = lax.broadcasted_iota(jnp.int32, (PEER_TOPK * PEER_TOPK, LANES), 0).astype(F32)
            for h, (s1, s2) in zip(heads, scores):
                store(h, *_route_exact(s1, s2, idx16, idx128, idx256))

        return carry

    lax.fori_loop(0, (rows // LANES) * groups, body, 0)


def _route(h2, wqt, keys):
    t = 2 * h2.shape[0]
    rows = min(ROUTE_ROWS, t)
    out = jax.ShapeDtypeStruct((PEER_HEADS, PEER_KEYS, t), F32)
    outw = jax.ShapeDtypeStruct((PEER_HEADS, PEER_KEYS // 2, t), jnp.uint32)
    ospec = pl.BlockSpec((PEER_HEADS, PEER_KEYS, rows), lambda i: (0, 0, i))
    wspec = pl.BlockSpec((PEER_HEADS, PEER_KEYS // 2, rows), lambda i: (0, 0, i))
    return pl.pallas_call(
        _route_kernel,
        out_shape=(out, out, outw, outw),
        grid=(t // rows,),
        in_specs=[
            pl.BlockSpec((rows // 2, D_MODEL), lambda i: (i, 0)),
            _const_spec((D_MODEL, D_MODEL)),
            _const_spec((2 * PEER_HEADS, PEER_KEYS, PEER_HALF)),
        ],
        out_specs=(ospec, ospec, wspec, wspec),
        scratch_shapes=[pltpu.VMEM((D_MODEL, rows), F32)],
        compiler_params=_params("parallel"),
        name="peer_route",
    )(h2, wqt, keys)


PACK = 16


def _gelu(a):
    return 0.5 * a * (1.0 + lax.erf(a * (2.0 ** -0.5)))


def _expert_weights(n1_ref, e1_ref, r2_ref, e2_ref, at_ref, hw_ref, tc, il):
    ls = slice(tc * LANES, (tc + 1) * LANES)
    gates = [None] * (PEER_KEYS // PACK)
    for h in range(PEER_HEADS):
        nb = jnp.broadcast_to(n1_ref[h, il:il + 1, ls], (PACK, LANES)).astype(BF16)
        eb = jnp.broadcast_to(e1_ref[h, il:il + 1, ls], (PACK, LANES)).astype(BF16)
        for k in range(PEER_KEYS // PACK):
            ws = slice(PACK // 2 * k, PACK // 2 * (k + 1))
            e2 = _unpack_words(e2_ref[h, ws, ls])
            keep = _unpack_words(r2_ref[h, ws, ls]) < nb
            if h == 0:
                gates[k] = jnp.where(keep, e2 * eb, jnp.zeros_like(e2))
            else:
                gates[k] = jnp.where(keep, gates[k] + e2 * eb, gates[k])
    for k in range(PEER_KEYS // PACK):
        e0 = il * PEER_KEYS + PACK * k
        act = _gelu(at_ref[e0:e0 + PACK, ls]).astype(BF16)
        hw_ref[e0 // 2:(e0 + PACK) // 2, ls] = pltpu.bitcast(gates[k] * act, jnp.uint32)


MXU_COLS = 256


def _peer_kernel(h2_ref, u_ref, vt_ref,
                 n1p_ref, e1p_ref, r2p_ref, e2p_ref, n1c_ref, e1c_ref, r2c_ref, e2c_ref,
                 x1_ref, g2_ref, y_ref, at0_scr, at1_scr, hw0_scr, hw1_scr, acc_scr,
                 *, pairs_per_block):
    g = pl.program_id(0)
    ec = PEER_CHUNK
    rows = x1_ref.shape[0]

    @pl.when(g == 0)
    def _():
        at1_scr[...] = jnp.zeros_like(at1_scr)
        hw0_scr[...] = jnp.zeros_like(hw0_scr)
        hw1_scr[...] = jnp.zeros_like(hw1_scr)
        acc_scr[...] = jnp.zeros_like(acc_scr)

    pair = jnp.maximum(g - 1, 0) % pairs_per_block

    def half_step(uw_row0, routing, at_w, at_r, hw_w, hw_r, vtw_col0, restart):
        for n in range(rows // MXU_COLS):
            ts = slice(n * MXU_COLS, (n + 1) * MXU_COLS)
            tw = slice(n * MXU_COLS // 2, (n + 1) * MXU_COLS // 2)
            at_w[:, ts] = _dot_nt(_unpack_words(u_ref[uw_row0:uw_row0 + ec // 2, :]),
                                  _unpack_words(h2_ref[tw, :]))
            for tc in range(n * (MXU_COLS // LANES), (n + 1) * (MXU_COLS // LANES)):
                for il in range(ec // PEER_KEYS):
                    _expert_weights(*routing, at_r, hw_w, tc, il)
            acc = acc_scr[:, ts]
            if restart is not None:
                acc = jnp.where(restart, 0.0, acc)
            acc_scr[:, ts] = acc + _dot(_unpack_words(vt_ref[:, vtw_col0:vtw_col0 + ec]),
                                        _unpack_words(hw_r[:, ts]))

    half_step(0, (n1p_ref, e1p_ref, r2p_ref, e2p_ref), at0_scr, at1_scr,
              hw1_scr, hw0_scr, 0, pair == 0)
    half_step(ec // 2, (n1c_ref, e1c_ref, r2c_ref, e2c_ref), at1_scr, at0_scr,
              hw0_scr, hw1_scr, ec, None)

    @pl.when((g > 0) & (pair == pairs_per_block - 1))
    def _():
        y_ref[...] = x1_ref[...] + g2_ref[...] * acc_scr[...].T


def _peer(h2, u, vt, n1, e1, r2, e2, x1, g2):
    t = x1.shape[0]
    rows = min(PEER_ROWS, t)
    n_chunks = PEER_EXPERTS // PEER_CHUNK
    ppb = n_chunks // 2
    n_pairs = (t // rows) * ppb
    n_items = 2 * n_pairs
    il_rows = PEER_CHUNK // PEER_KEYS

    pair_a = lambda g: jnp.minimum(g, n_pairs - 1)
    item_p = lambda g: jnp.maximum(2 * g - 1, 0)
    item_c = lambda g: jnp.minimum(2 * g, n_items - 1)
    block_c = lambda g: jnp.maximum(g - 1, 0) // ppb

    def routing_specs(item):
        per_chunk = pl.BlockSpec((PEER_HEADS, il_rows, rows),
                                 lambda g: (0, item(g) % n_chunks, item(g) // n_chunks))
        per_block = pl.BlockSpec((PEER_HEADS, PEER_KEYS // 2, rows),
                                 lambda g: (0, 0, item(g) // n_chunks))
        return [per_chunk, per_chunk, per_block, per_block]

    if g2.shape[1] == 1:
        blocks_per_seq = (t // g2.shape[0]) // rows
        g2_spec = pl.BlockSpec((None, 1, D_MODEL), lambda g: (block_c(g) // blocks_per_seq, 0, 0))
    else:
        g2_spec = pl.BlockSpec((None, rows, D_MODEL), lambda g: (0, block_c(g), 0))
    tok_c = pl.BlockSpec((rows, D_MODEL), lambda g: (block_c(g), 0))
    return pl.pallas_call(
        functools.partial(_peer_kernel, pairs_per_block=ppb),
        out_shape=jax.ShapeDtypeStruct((t, D_MODEL), F32),
        grid=(n_pairs + 1,),
        in_specs=[
            pl.BlockSpec((rows // 2, D_MODEL), lambda g: (pair_a(g) // ppb, 0)),
            pl.BlockSpec((PEER_CHUNK, D_MODEL), lambda g: (pair_a(g) % ppb, 0)),
            pl.BlockSpec((D_MODEL // 2, 2 * PEER_CHUNK), lambda g: (0, jnp.maximum(g - 1, 0) % ppb)),
            *routing_specs(item_p), *routing_specs(item_c),
            tok_c, g2_spec,
        ],
        out_specs=tok_c,
        scratch_shapes=[pltpu.VMEM((PEER_CHUNK, rows), F32),
                        pltpu.VMEM((PEER_CHUNK, rows), F32),
                        pltpu.VMEM((PEER_CHUNK // 2, rows), jnp.uint32),
                        pltpu.VMEM((PEER_CHUNK // 2, rows), jnp.uint32),
                        pltpu.VMEM((D_MODEL, rows), F32)],
        compiler_params=_params("arbitrary"),
        name="peer_experts",
    )(h2, u, vt, n1, e1, r2, e2, n1, e1, r2, e2, x1, g2)


def _dup_heads(w):
    lead = w.shape[:-1]
    w = w.reshape(lead + (N_KV_HEADS, 1, HEAD_DIM))
    return jnp.broadcast_to(w, lead + (N_KV_HEADS, 2, HEAD_DIM)).reshape(lead + (2 * KV_W,))


def _first_of_dup(a):
    return a.reshape(a.shape[:-1] + (N_KV_HEADS, 2, HEAD_DIM))[..., 0, :]


def kernel(x_prompt, x_sample, cache_k, cache_v, state_conv, c_prompt, c_sample, w_ada, b_ada, norm1_w, w_in, q_norm_w, k_norm_w, sinks, conv_w, conv_bias, w_out, norm2_w, w_query, sub_keys, expert_u, expert_v):
    depth = w_ada.shape[0]
    batch, seq, _ = x_prompt.shape
    nseq, dec = x_sample.shape[:2]
    assert dec <= SAMPLE_Q_ROWS and dec >= CONV_W - 1
    ts = nseq * dec

    head_of_col = jnp.arange(D_MODEL, dtype=jnp.int32) // HEAD_DIM
    hsum = (head_of_col[:, None] == jnp.arange(LANES, dtype=jnp.int32)[None, :]).astype(BF16)
    hexp = hsum.T
    c_all = jnp.concatenate([c_prompt, c_sample], axis=0)
    c_all = jnp.pad(c_all, ((0, (-c_all.shape[0]) % 8), (0, 0)))

    yp = x_prompt
    ys = x_sample.reshape(1, ts, D_MODEL)
    kp, vp, cp, ksm, vsm, csm = [], [], [], [], [], []
    for l in range(depth):
        win = w_in[l]
        win2 = jnp.concatenate(
            [win[:, :D_MODEL], _dup_heads(win[:, D_MODEL:D_MODEL + KV_W]),
             _dup_heads(win[:, D_MODEL + KV_W:D_MODEL + 2 * KV_W]), win[:, D_MODEL + 2 * KV_W:]],
            axis=1).astype(BF16)
        nw1 = norm1_w[l].reshape(1, D_MODEL)
        nw2 = norm2_w[l].reshape(1, D_MODEL)
        qw = jnp.tile(q_norm_w[l], N_HEADS).reshape(1, D_MODEL)
        kw = jnp.tile(k_norm_w[l], 2 * N_KV_HEADS).reshape(1, 2 * KV_W)
        cw = conv_w[l]
        cb = conv_bias[l].reshape(1, D_MODEL)
        wout = w_out[l].astype(BF16)
        wqt = w_query[l].T.astype(BF16)
        keys = sub_keys[l].reshape(2 * PEER_HEADS, PEER_KEYS, PEER_HALF)
        u = _pack_table(expert_u[l])
        vt = _pack_table_transposed(expert_v[l])

        mod = _adaln(c_all, w_ada[l], b_ada[l])
        mod_p = [mod[j, :batch].reshape(batch, 1, D_MODEL) for j in range(6)]
        mod_s = [jnp.repeat(mod[j, batch:batch + nseq], dec, axis=0).reshape(1, ts, D_MODEL)
                 for j in range(6)]

        q, k2, v2, cu, gc, sa = _project(yp, mod_p[1], mod_p[0], nw1, win2, qw, kw, hsum, hexp)
        x1, h2 = _prompt_mix(sinks[l], yp, q, k2, v2, cu, gc, sa, mod_p[2], mod_p[4], mod_p[3],
                             cw, cb, wout, nw2)
        h2 = h2.reshape(batch * seq // 2, D_MODEL)
        n1, e1, r2, e2 = _route(h2, wqt, keys)
        yp = _peer(h2, u, vt, n1, e1, r2, e2, x1.reshape(batch * seq, D_MODEL), mod_p[5])
        yp = yp.reshape(batch, seq, D_MODEL)
        kp.append(_first_of_dup(k2[:, seq - WINDOW:]))
        vp.append(_first_of_dup(v2[:, seq - WINDOW:]))
        cp.append(cu[:, seq - (CONV_W - 1):])

        q, k2, v2, cu, gc, sa = _project(ys, mod_s[1], mod_s[0], nw1, win2, qw, kw, hsum, hexp)
        pad_rows = lambda a: jnp.pad(a.reshape(nseq, dec, a.shape[-1]),
                                     ((0, 0), (0, SAMPLE_Q_ROWS - dec), (0, 0)))
        oat = _sample_attn(sinks[l], pad_rows(q[0]),
                           _dup_heads(cache_k[l].reshape(nseq, WINDOW, KV_W)).astype(BF16),
                           pad_rows(k2[0]).astype(BF16),
                           _dup_heads(cache_v[l].reshape(nseq, WINDOW, KV_W)).astype(BF16),
                           pad_rows(v2[0]).astype(BF16))
        oat = oat[:, :dec].reshape(ts, D_MODEL)
        cu_seq = cu.reshape(nseq, dec, D_MODEL)
        up = jnp.concatenate([state_conv[l], cu_seq], axis=1)
        prev2 = up[:, 0:dec].reshape(ts, D_MODEL)
        prev1 = up[:, 1:dec + 1].reshape(ts, D_MODEL)
        flat = lambda a: a.reshape(ts, D_MODEL)
        x1, h2 = _sample_mix(flat(ys), flat(cu), prev1, prev2, flat(gc), flat(sa), oat,
                             flat(mod_s[2]), flat(mod_s[4]), flat(mod_s[3]), cw, cb, wout, nw2)
        n1, e1, r2, e2 = _route(h2, wqt, keys)
        ys = _peer(h2, u, vt, n1, e1, r2, e2, x1, mod_s[5]).reshape(1, ts, D_MODEL)
        k_new = _first_of_dup(k2[0]).reshape(nseq, dec, N_KV_HEADS, HEAD_DIM)
        v_new = _first_of_dup(v2[0]).reshape(nseq, dec, N_KV_HEADS, HEAD_DIM)
        ksm.append(jnp.concatenate([cache_k[l], k_new], axis=1)[:, -WINDOW:])
        vsm.append(jnp.concatenate([cache_v[l], v_new], axis=1)[:, -WINDOW:])
        csm.append(up[:, -(CONV_W - 1):])

    return (yp, ys.reshape(nseq, dec, D_MODEL), jnp.stack(kp), jnp.stack(vp), jnp.stack(cp),
            jnp.stack(ksm), jnp.stack(vsm), jnp.stack(csm))
```

```python
import functools

import jax
import jax.numpy as jnp
from jax import lax
from jax.experimental import pallas as pl
from jax.experimental.pallas import tpu as pltpu

F32 = jnp.float32
BF16 = jnp.bfloat16

D_MODEL = 1024
N_HEADS = 16
N_KV_HEADS = 4
HEAD_DIM = 64
WINDOW = 128
KV_W = N_KV_HEADS * HEAD_DIM
CONV_W = 3
PEER_HEADS = 8
PEER_KEYS = 128
PEER_TOPK = 16
PEER_HALF = 64
PEER_EXPERTS = PEER_KEYS * PEER_KEYS
RMS_EPS = 1e-6

LANES = 128
VMEM_LIMIT = 56 * 1024 * 1024

PROJ_ROWS = 512
ATTN_ROWS = 512
PACK_ROWS = 1024
ROUTE_ROWS = 512
ROUTE_HEADS = 4
PEER_ROWS = 512
PEER_CHUNK = 1024
SAMPLE_Q_ROWS = 8
SAMPLE_SEQ_BLOCK = 8

C_Q = 0
C_K = C_Q + D_MODEL
C_V = C_K + 2 * KV_W
C_U = C_V + 2 * KV_W
C_GPOST = C_U + D_MODEL
C_GPRE = C_GPOST + D_MODEL
C_GATTN = C_GPRE + D_MODEL
C_GCONV = C_GATTN + D_MODEL
C_END = C_GCONV + D_MODEL


def _params(*semantics):
    return pltpu.CompilerParams(dimension_semantics=semantics, vmem_limit_bytes=VMEM_LIMIT)


def _const_spec(shape):
    nd = len(shape)
    return pl.BlockSpec(shape, lambda *_: (0,) * nd, pipeline_mode=pl.Buffered(1))


def _mod_spec(arr, rows):
    if arr.shape[1] == 1:
        return pl.BlockSpec((None, 1, D_MODEL), lambda b, i, *_: (b, 0, 0))
    return pl.BlockSpec((None, rows, D_MODEL), lambda b, i, *_: (b, i, 0))


def _dot(a, b):
    return jnp.dot(a, b, preferred_element_type=F32)


def _dot_nt(a, b):
    return lax.dot_general(a, b, (((1,), (1,)), ((), ())), preferred_element_type=F32)


def _pack_words(x):
    return pltpu.bitcast(x.astype(BF16), jnp.uint32)


def _unpack_words(w):
    return pltpu.bitcast(w, BF16)


def _dot_split(a, b):
    hi = a.astype(BF16)
    lo = (a - hi.astype(F32)).astype(BF16)
    return _dot(hi, b) + _dot(lo, b)


def _ada_kernel(c_ref, w_ref, b_ref, o_ref):
    s = jax.nn.silu(c_ref[...])
    o_ref[...] = jnp.dot(s, w_ref[...], precision=lax.Precision.HIGHEST,
                         preferred_element_type=F32) + b_ref[...]


def _adaln(c_all, w_ada, b_ada):
    rows = c_all.shape[0]
    return pl.pallas_call(
        _ada_kernel,
        out_shape=jax.ShapeDtypeStruct((6, rows, D_MODEL), F32),
        grid=(6,),
        in_specs=[
            pl.BlockSpec((rows, D_MODEL), lambda j: (0, 0)),
            pl.BlockSpec((D_MODEL, D_MODEL), lambda j: (0, j)),
            pl.BlockSpec((1, D_MODEL), lambda j: (0, j)),
        ],
        out_specs=pl.BlockSpec((None, rows, D_MODEL), lambda j: (j, 0, 0)),
        compiler_params=_params("arbitrary"),
        name="adaln",
    )(c_all, w_ada, b_ada.reshape(1, 6 * D_MODEL))


def _pack_kernel(x_ref, o_ref):
    o_ref[...] = _pack_words(x_ref[...])


def _pack_t_kernel(x_ref, o_ref):
    o_ref[...] = _pack_words(x_ref[...].T)


def _pack_table(x):
    n, m = x.shape
    return pl.pallas_call(
        _pack_kernel,
        out_shape=jax.ShapeDtypeStruct((n // 2, m), jnp.uint32),
        grid=(n // PACK_ROWS,),
        in_specs=[pl.BlockSpec((PACK_ROWS, m), lambda i: (i, 0))],
        out_specs=pl.BlockSpec((PACK_ROWS // 2, m), lambda i: (i, 0)),
        compiler_params=_params("parallel"),
        name="pack_table",
    )(x)


def _pack_table_transposed(x):
    n, m = x.shape
    return pl.pallas_call(
        _pack_t_kernel,
        out_shape=jax.ShapeDtypeStruct((m // 2, n), jnp.uint32),
        grid=(n // PACK_ROWS,),
        in_specs=[pl.BlockSpec((PACK_ROWS, m), lambda i: (i, 0))],
        out_specs=pl.BlockSpec((m // 2, PACK_ROWS), lambda i: (0, i)),
        compiler_params=_params("parallel"),
        name="pack_table_t",
    )(x)


def _head_rms(z, hsum, hexp, w):
    ss = _dot((z * z).astype(BF16), hsum)
    r = lax.rsqrt(ss * (1.0 / HEAD_DIM) + RMS_EPS)
    return z * _dot_split(r, hexp) * w


def _proj_kernel(x_ref, sc_ref, sh_ref, nw_ref, win_ref, qw_ref, kw_ref, hsum_ref, hexp_ref,
                 q_ref, k2_ref, v2_ref, cu_ref, gc_ref, sa_ref):
    x = x_ref[...]
    h = x * lax.rsqrt(jnp.mean(x * x, axis=-1, keepdims=True) + RMS_EPS) * nw_ref[...]
    h = h * (1.0 + sc_ref[...]) + sh_ref[...]
    hb = h.astype(BF16)

    def seg(lo, hi):
        return _dot(hb, win_ref[:, lo:hi])

    q = _head_rms(seg(C_Q, C_K), hsum_ref[...], hexp_ref[...], qw_ref[...])
    q_ref[...] = (q * (HEAD_DIM ** -0.5)).astype(BF16)
    nk = 2 * KV_W
    k2_ref[...] = _head_rms(seg(C_K, C_V), hsum_ref[:nk, :], hexp_ref[:, :nk], kw_ref[...])
    v2_ref[...] = seg(C_V, C_U)
    cu_ref[...] = seg(C_GPRE, C_GATTN) * seg(C_U, C_GPOST)
    gc_ref[...] = (seg(C_GPOST, C_GPRE) * jax.nn.sigmoid(seg(C_GCONV, C_END))).astype(BF16)
    sa_ref[...] = jax.nn.sigmoid(seg(C_GATTN, C_GCONV)).astype(BF16)


def _project(x, sc1, sh1, nw, win2, qw, kw, hsum, hexp):
    nb, s, _ = x.shape
    rows = min(PROJ_ROWS, s)
    tok = lambda w, dt: jax.ShapeDtypeStruct((nb, s, w), dt)
    tspec = lambda w: pl.BlockSpec((None, rows, w), lambda b, i: (b, i, 0))
    return pl.pallas_call(
        _proj_kernel,
        out_shape=(tok(D_MODEL, BF16), tok(2 * KV_W, F32), tok(2 * KV_W, F32),
                   tok(D_MODEL, F32), tok(D_MODEL, BF16), tok(D_MODEL, BF16)),
        grid=(nb, s // rows),
        in_specs=[
            tspec(D_MODEL), _mod_spec(sc1, rows), _mod_spec(sh1, rows),
            _const_spec((1, D_MODEL)), _const_spec((D_MODEL, C_END)),
            _const_spec((1, D_MODEL)), _const_spec((1, 2 * KV_W)),
            _const_spec((D_MODEL, LANES)), _const_spec((LANES, D_MODEL)),
        ],
        out_specs=(tspec(D_MODEL), tspec(2 * KV_W), tspec(2 * KV_W),
                   tspec(D_MODEL), tspec(D_MODEL), tspec(D_MODEL)),
        compiler_params=_params("parallel", "parallel"),
        name="in_proj",
    )(x, sc1, sh1, nw, win2, qw, kw, hsum, hexp)


def _attend_blocks(blocks, sinks_ref):
    qb = blocks[0][0].shape[0]
    lane = lax.broadcasted_iota(jnp.int32, (qb, LANES), 1)
    low = lane < HEAD_DIM
    row = lax.broadcasted_iota(jnp.int32, (4 * qb, 2 * WINDOW), 0)
    col = lax.broadcasted_iota(jnp.int32, (4 * qb, 2 * WINDOW), 1)
    dist = col - (row & (qb - 1))
    band = (dist >= 1) & (dist <= WINDOW)
    blk = lax.broadcasted_iota(jnp.int32, (4 * qb, 1), 0) // qb
    ones = jnp.ones((2 * WINDOW, LANES), BF16)
    sinks = []
    for j in range(N_KV_HEADS):
        sink = jnp.full((4 * qb, 1), sinks_ref[4 * j + 3], F32)
        for g in range(2, -1, -1):
            sink = jnp.where(blk == g, sinks_ref[4 * j + g], sink)
        sinks.append(sink)

    scores = []
    for qblk, k2, _, _ in blocks:
        for j in range(N_KV_HEADS):
            qa = qblk[:, 2 * j * LANES:(2 * j + 1) * LANES].astype(F32)
            qc = qblk[:, (2 * j + 1) * LANES:(2 * j + 2) * LANES].astype(F32)
            lhs = jnp.concatenate([jnp.where(low, qa, 0.0), jnp.where(low, 0.0, qa),
                                   jnp.where(low, qc, 0.0), jnp.where(low, 0.0, qc)], axis=0)
            scores.append(_dot_nt(lhs.astype(BF16), k2[:, j * LANES:(j + 1) * LANES]))

    probs = []
    for b, (_, _, _, first_col) in enumerate(blocks):
        mask = band if isinstance(first_col, int) and first_col == 0 else band & (col >= first_col)
        for j in range(N_KV_HEADS):
            s = jnp.where(mask, scores[b * N_KV_HEADS + j], -jnp.inf)
            m = jnp.maximum(jnp.max(s, axis=-1, keepdims=True), sinks[j])
            probs.append((jnp.exp(s - m).astype(BF16), jnp.exp(sinks[j] - m)))

    results = []
    for b, (_, _, v2, _) in enumerate(blocks):
        outs = []
        for j in range(N_KV_HEADS):
            p, sink_term = probs[b * N_KV_HEADS + j]
            o = _dot(p, jnp.concatenate([v2[:, j * LANES:(j + 1) * LANES], ones], axis=1))
            on = o[:, :LANES] / (o[:, LANES:] + sink_term)
            outs.append(jnp.where(low, on[0:qb], on[qb:2 * qb]))
            outs.append(jnp.where(low, on[2 * qb:3 * qb], on[3 * qb:4 * qb]))
        results.append(jnp.concatenate(outs, axis=1))
    return results


def _merge_tail(x, cu, prev1, prev2, gc, sa, oattn, g1, sc2, sh2, cw_ref, cb, wout, n2w):
    conv_y = prev2 * cw_ref[0:1, :] + prev1 * cw_ref[1:2, :] + cu * cw_ref[2:3, :] + cb
    merged = sa.astype(F32) * oattn + gc.astype(F32) * conv_y
    x1 = x + g1 * _dot(merged.astype(BF16), wout)
    h2 = x1 * lax.rsqrt(jnp.mean(x1 * x1, axis=-1, keepdims=True) + RMS_EPS) * n2w
    h2 = h2 * (1.0 + sc2) + sh2
    return x1, _pack_words(h2)


def _prompt_mix_kernel(sinks_ref, x_ref, q_ref, k_ref, kp_ref, v_ref, vp_ref, cu_ref, cup_ref,
                       gc_ref, sa_ref, g1_ref, sc2_ref, sh2_ref, cw_ref, cb_ref, wout_ref, n2w_ref,
                       x1_ref, h2_ref, oat_scr):
    i = pl.program_id(1)
    rows = x_ref.shape[0]
    kext = jnp.concatenate([kp_ref[...], k_ref[...]], axis=0).astype(BF16)
    vext = jnp.concatenate([vp_ref[...], v_ref[...]], axis=0).astype(BF16)
    for w in range(rows // WINDOW):
        first_col = jnp.where(i == 0, WINDOW, 0) if w == 0 else 0
        lo = w * WINDOW
        oat_scr[lo:lo + WINDOW, :] = _attend_blocks(
            [(q_ref[lo:lo + WINDOW, :], kext[lo:lo + 2 * WINDOW, :], vext[lo:lo + 2 * WINDOW, :],
              first_col)], sinks_ref)[0]

    cu = cu_ref[...]
    has_prev = (i > 0).astype(F32)
    p1 = cup_ref[7:8, :] * has_prev
    p2 = cup_ref[6:7, :] * has_prev
    r = lax.broadcasted_iota(jnp.int32, (rows, 1), 0)
    prev1 = jnp.where(r == 0, p1, pltpu.roll(cu, 1, axis=0))
    prev2 = jnp.where(r == 0, p2, jnp.where(r == 1, p1, pltpu.roll(cu, 2, axis=0)))
    x1, h2 = _merge_tail(x_ref[...], cu, prev1, prev2, gc_ref[...], sa_ref[...], oat_scr[...],
                         g1_ref[...], sc2_ref[...], sh2_ref[...], cw_ref, cb_ref[...],
                         wout_ref[...], n2w_ref[...])
    x1_ref[...] = x1
    h2_ref[...] = h2


def _prompt_mix(sinks, x, q, k2, v2, cu, gc, sa, g1, sc2, sh2, cw, cb, wout, n2w):
    nb, s, _ = x.shape
    rows = ATTN_ROWS
    wpt = rows // WINDOW
    tspec = lambda w: pl.BlockSpec((None, rows, w), lambda b, i: (b, i, 0))
    prev_win = pl.BlockSpec((None, WINDOW, 2 * KV_W), lambda b, i: (b, jnp.maximum(i * wpt - 1, 0), 0))
    prev_rows = pl.BlockSpec((None, 8, D_MODEL), lambda b, i: (b, jnp.maximum(i * (rows // 8) - 1, 0), 0))
    mod = pl.BlockSpec((None, 1, D_MODEL), lambda b, i: (b, 0, 0))
    return pl.pallas_call(
        _prompt_mix_kernel,
        out_shape=(jax.ShapeDtypeStruct((nb, s, D_MODEL), F32),
                   jax.ShapeDtypeStruct((nb, s // 2, D_MODEL), jnp.uint32)),
        grid=(nb, s // rows),
        in_specs=[
            pl.BlockSpec(memory_space=pltpu.SMEM),
            tspec(D_MODEL), tspec(D_MODEL),
            tspec(2 * KV_W), prev_win, tspec(2 * KV_W), prev_win,
            tspec(D_MODEL), prev_rows, tspec(D_MODEL), tspec(D_MODEL),
            mod, mod, mod,
            _const_spec((CONV_W, D_MODEL)), _const_spec((1, D_MODEL)),
            _const_spec((D_MODEL, D_MODEL)), _const_spec((1, D_MODEL)),
        ],
        out_specs=(tspec(D_MODEL), pl.BlockSpec((None, rows // 2, D_MODEL), lambda b, i: (b, i, 0))),
        scratch_shapes=[pltpu.VMEM((rows, D_MODEL), F32)],
        compiler_params=_params("parallel", "parallel"),
        name="prompt_mix",
    )(sinks, x, q, k2, k2, v2, v2, cu, cu, gc, sa, g1, sc2, sh2, cw, cb, wout, n2w)


def _sample_attn_kernel(sinks_ref, q_ref, ck_ref, kn_ref, cv_ref, vn_ref, o_ref):
    pad = jnp.zeros((WINDOW - SAMPLE_Q_ROWS, 2 * KV_W), BF16)
    blocks = []
    for n in range(q_ref.shape[0]):
        kext = jnp.concatenate([ck_ref[n], kn_ref[n], pad], axis=0)
        vext = jnp.concatenate([cv_ref[n], vn_ref[n], pad], axis=0)
        blocks.append((q_ref[n], kext, vext, 0))
    for n, o in enumerate(_attend_blocks(blocks, sinks_ref)):
        o_ref[n] = o


def _sample_attn(sinks, q8, ck2, k2n, cv2, v2n):
    nseq = q8.shape[0]
    sb = SAMPLE_SEQ_BLOCK
    spec = lambda r, w: pl.BlockSpec((sb, r, w), lambda n: (n, 0, 0))
    return pl.pallas_call(
        _sample_attn_kernel,
        out_shape=jax.ShapeDtypeStruct((nseq, SAMPLE_Q_ROWS, D_MODEL), F32),
        grid=(nseq // sb,),
        in_specs=[
            pl.BlockSpec(memory_space=pltpu.SMEM),
            spec(SAMPLE_Q_ROWS, D_MODEL),
            spec(WINDOW, 2 * KV_W), spec(SAMPLE_Q_ROWS, 2 * KV_W),
            spec(WINDOW, 2 * KV_W), spec(SAMPLE_Q_ROWS, 2 * KV_W),
        ],
        out_specs=spec(SAMPLE_Q_ROWS, D_MODEL),
        compiler_params=_params("parallel"),
        name="sample_attn",
    )(sinks, q8, ck2, k2n, cv2, v2n)


def _sample_mix_kernel(x_ref, cu_ref, p1_ref, p2_ref, gc_ref, sa_ref, oat_ref,
                       g1_ref, sc2_ref, sh2_ref, cw_ref, cb_ref, wout_ref, n2w_ref,
                       x1_ref, h2_ref):
    x1, h2 = _merge_tail(x_ref[...], cu_ref[...], p1_ref[...], p2_ref[...], gc_ref[...],
                         sa_ref[...], oat_ref[...], g1_ref[...], sc2_ref[...], sh2_ref[...],
                         cw_ref, cb_ref[...], wout_ref[...], n2w_ref[...])
    x1_ref[...] = x1
    h2_ref[...] = h2


def _sample_mix(x, cu, prev1, prev2, gc, sa, oattn, g1, sc2, sh2, cw, cb, wout, n2w):
    t = x.shape[0]
    full = lambda: pl.BlockSpec((t, D_MODEL), lambda: (0, 0))
    return pl.pallas_call(
        _sample_mix_kernel,
        out_shape=(jax.ShapeDtypeStruct((t, D_MODEL), F32),
                   jax.ShapeDtypeStruct((t // 2, D_MODEL), jnp.uint32)),
        in_specs=[full() for _ in range(10)] + [
            pl.BlockSpec((CONV_W, D_MODEL), lambda: (0, 0)),
            pl.BlockSpec((1, D_MODEL), lambda: (0, 0)),
            pl.BlockSpec((D_MODEL, D_MODEL), lambda: (0, 0)),
            pl.BlockSpec((1, D_MODEL), lambda: (0, 0)),
        ],
        out_specs=(full(), pl.BlockSpec((t // 2, D_MODEL), lambda: (0, 0))),
        compiler_params=pltpu.CompilerParams(vmem_limit_bytes=VMEM_LIMIT),
        name="sample_mix",
    )(x, cu, prev1, prev2, gc, sa, oattn, g1, sc2, sh2, cw, cb, wout, n2w)


def _top16(s, index):
    rank = jnp.full(s.shape, float(PEER_TOPK), F32)
    vals = []
    for r in range(PEER_TOPK):
        m = jnp.max(s, axis=0, keepdims=True)
        first = jnp.min(jnp.where(s == m, index, float(s.shape[0])), axis=0, keepdims=True)
        hit = index == first
        rank = jnp.where(hit, float(r), rank)
        s = jnp.where(hit, -jnp.inf, s)
        vals.append(m)
    return vals, rank


def _row_array(rows, index):
    out = jnp.zeros(index.shape, F32)
    for r, v in enumerate(rows):
        out = jnp.where(index == float(r), v, out)
    return out


def _gates(s1, s2, v1_0, v2_0, z):
    return jnp.exp(s1 - v1_0) / z, jnp.exp(s2 - v2_0)


def _route_exact(s1, s2, idx16, idx128, idx256):
    v1, rank1 = _top16(s1, idx128)
    v2, rank2 = _top16(s2, idx128)
    v2a = _row_array(v2, idx16)
    cand = jnp.concatenate([v1[r] + v2a for r in range(PEER_TOPK)], axis=0)
    sel = jnp.zeros(cand.shape, F32)
    c = cand
    for _ in range(PEER_TOPK):
        m = jnp.max(c, axis=0, keepdims=True)
        first = jnp.min(jnp.where(c == m, idx256, float(PEER_TOPK * PEER_TOPK)),
                        axis=0, keepdims=True)
        hit = idx256 == first
        sel = jnp.where(hit, 1.0, sel)
        c = jnp.where(hit, -jnp.inf, c)
    z = jnp.sum(sel * jnp.exp(cand - (v1[0] + v2[0])), axis=0, keepdims=True)
    n1 = jnp.zeros(rank1.shape, F32)
    for r in range(PEER_TOPK):
        n_r = jnp.sum(sel[r * PEER_TOPK:(r + 1) * PEER_TOPK], axis=0, keepdims=True)
        n1 = jnp.where(rank1 == float(r), n_r, n1)
    e1, e2 = _gates(s1, s2, v1[0], v2[0], z)
    return n1, e1, rank2, e2


SUBLANES = 8


def _batcher_pairs(n):
    pairs = []
    p = 1
    while p < n:
        k = p
        while k >= 1:
            for j in range(k % p, n - k, 2 * k):
                for i in range(min(k, n - j - k)):
                    if (i + j) // (2 * p) == (i + j + k) // (2 * p):
                        pairs.append((i + j, i + j + k))
            k //= 2
        p *= 2
    return pairs


def _sort_desc(vs):
    vs = list(vs)
    for i, j in _batcher_pairs(len(vs)):
        vs[i], vs[j] = jnp.maximum(vs[i], vs[j]), jnp.minimum(vs[i], vs[j])
    return vs


def _merge_top(a, b):
    n = len(a)
    c = [jnp.maximum(a[i], b[n - 1 - i]) for i in range(n)]
    d = n // 2
    while d >= 1:
        for i in range(n):
            if i & d == 0:
                c[i], c[i + d] = jnp.maximum(c[i], c[i + d]), jnp.minimum(c[i], c[i + d])
        d //= 2
    return c


def _merge_sublanes(w):
    for shift in (4, 2, 1):
        w = _merge_top(w, [pltpu.roll(x, shift, axis=0) for x in w])
    return w


def _tiles(s):
    return [s[SUBLANES * j:SUBLANES * (j + 1)] for j in range(s.shape[0] // SUBLANES)]


def _has_adjacent_tie(v):
    gap = v[0] - v[1]
    for r in range(1, len(v) - 1):
        gap = jnp.minimum(gap, v[r] - v[r + 1])
    return gap == 0.0


def _count_rows(flags):
    acc = flags[0]
    for f in flags[1:]:
        acc = acc + f
    return jnp.sum(acc, axis=0, keepdims=True)


def _route_distinct(s1, s2, sub):
    k = PEER_TOPK
    t1, t2 = _tiles(s1), _tiles(s2)
    v1 = _merge_sublanes(_sort_desc(t1))
    v2 = _merge_sublanes(_sort_desc(t2))

    pick = lambda v: jnp.where(sub < 2.0, v[0], jnp.where(sub < 4.0, v[1], jnp.where(sub < 6.0, v[2], v[3])))
    a_const, b_const = pick(v1), pick(v2)
    lengths = (16, 15, 7, 6, 3, 2, 1)
    lists = []
    for i in range(k):
        a, b = a_const, b_const
        for s_col, s_row, off in ((5.0, 4.0, 2), (3.0, 2.0, 1), (1.0, 0.0, 0)):
            if i < lengths[int(s_col)]:
                a = jnp.where(sub == s_col, v1[i + off + 1], a)
            if i < lengths[int(s_row)]:
                b = jnp.where(sub == s_row, v2[i + off], b)
        live = float(sum(1 for n in lengths if n > i))
        lists.append(jnp.where(sub < live, a + b, -jnp.inf))
    c = _merge_sublanes(lists)
    tau = c[k - 1]

    z = None
    for x in lists:
        e = jnp.where(x >= tau, jnp.exp(x - c[0]), 0.0)
        z = e if z is None else z + e
    z = jnp.sum(z, axis=0, keepdims=True)

    counts, total = [], None
    for r1 in range(k):
        n_r = jnp.zeros_like(tau)
        for r2 in range(k // (r1 + 1)):
            n_r = jnp.where(v1[r1] + v2[r2] >= tau, float(r2 + 1), n_r)
        counts.append(n_r)
        total = n_r if total is None else total + n_r

    n1, rank2, lead1, lead2 = [], [], [], []
    for x1, x2 in zip(t1, t2):
        n = jnp.zeros_like(x1)
        for r in range(k - 1, -1, -1):
            n = jnp.where(x1 >= v1[r], counts[r], n)
        rk = jnp.zeros_like(x2)
        for r in range(k):
            rk = jnp.where(v2[r] > x2, float(r + 1), rk)
        n1.append(n)
        rank2.append(rk)
        lead1.append(jnp.where(x1 >= v1[k - 1], 1.0, 0.0))
        lead2.append(jnp.where(x2 >= v2[k - 1], 1.0, 0.0))
    n1 = jnp.concatenate(n1, axis=0)
    rank2 = jnp.concatenate(rank2, axis=0)

    want = float(k)
    tie = _has_adjacent_tie(v1) | _has_adjacent_tie(v2) | _has_adjacent_tie(c)
    valid = (~tie) & (_count_rows(lead1) == want) & (_count_rows(lead2) == want) & (total == want)
    e1, e2 = _gates(s1, s2, v1[0][0:1], v2[0][0:1], z)
    return n1, e1, rank2, e2, valid


def _route_kernel(h2_ref, wqt_ref, keys_ref, n1_ref, e1_ref, r2_ref, e2_ref, qpt_scr):
    rows = 2 * h2_ref.shape[0]
    qpt_scr[...] = _dot_nt(wqt_ref[...], _unpack_words(h2_ref[...]))
    idx8 = lax.broadcasted_iota(jnp.int32, (8, LANES), 0).astype(F32)
    idx16 = lax.broadcasted_iota(jnp.int32, (PEER_TOPK, LANES), 0).astype(F32)

    groups = PEER_HEADS // ROUTE_HEADS

    def body(it, carry):
        tc = it // groups
        ls = pl.ds(pl.multiple_of(tc * LANES, LANES), LANES)
        hp = lax.Precision.HIGHEST
        heads, scores = [], []
        for j in range(ROUTE_HEADS):
            h = (it % groups) * ROUTE_HEADS + j
            base = pl.multiple_of(h * (2 * PEER_HALF), 2 * PEER_HALF)
            q1 = qpt_scr[pl.ds(base, PEER_HALF), ls]
            q2 = qpt_scr[pl.ds(base + PEER_HALF, PEER_HALF), ls]
            s1 = jnp.dot(keys_ref[2 * h], q1, precision=hp, preferred_element_type=F32)
            s2 = jnp.dot(keys_ref[2 * h + 1], q2, precision=hp, preferred_element_type=F32)
            heads.append(h)
            scores.append((s1, s2))

        def store(h, n1, e1, rank2, e2):
            n1_ref[h, :, ls] = n1
            e1_ref[h, :, ls] = e1
            r2_ref[h, :, ls] = _pack_words(rank2)
            e2_ref[h, :, ls] = _pack_words(e2)

        tied = []
        for h, (s1, s2) in zip(heads, scores):
            n1, e1, rank2, e2, valid = _route_distinct(s1, s2, idx8)
            store(h, n1, e1, rank2, e2)
            tied.append(jnp.max(jnp.where(valid, 0.0, 1.0)) > 0.0)

        for h, (s1, s2), redo in zip(heads, scores, tied):
            @pl.when(redo)
            def _():
                idx128 = lax.broadcasted_iota(jnp.int32, (PEER_KEYS, LANES), 0).astype(F32)
                idx256 = lax.broadcasted_iota(jnp.int32, (PEER_TOPK * PEER_TOPK, LANES), 0).astype(F32)
                store(h, *_route_exact(s1, s2, idx16, idx128, idx256))

        return carry

    lax.fori_loop(0, (rows // LANES) * groups, body, 0)


def _route(h2, wqt, keys):
    t = 2 * h2.shape[0]
    rows = min(ROUTE_ROWS, t)
    out = jax.ShapeDtypeStruct((PEER_HEADS, PEER_KEYS, t), F32)
    outw = jax.ShapeDtypeStruct((PEER_HEADS, PEER_KEYS // 2, t), jnp.uint32)
    ospec = pl.BlockSpec((PEER_HEADS, PEER_KEYS, rows), lambda i: (0, 0, i))
    wspec = pl.BlockSpec((PEER_HEADS, PEER_KEYS // 2, rows), lambda i: (0, 0, i))
    return pl.pallas_call(
        _route_kernel,
        out_shape=(out, out, outw, outw),
        grid=(t // rows,),
        in_specs=[
            pl.BlockSpec((rows // 2, D_MODEL), lambda i: (i, 0)),
            _const_spec((D_MODEL, D_MODEL)),
            _const_spec((2 * PEER_HEADS, PEER_KEYS, PEER_HALF)),
        ],
        out_specs=(ospec, ospec, wspec, wspec),
        scratch_shapes=[pltpu.VMEM((D_MODEL, rows), F32)],
        compiler_params=_params("parallel"),
        name="peer_route",
    )(h2, wqt, keys)


PACK = 16


def _gelu(a):
    return 0.5 * a * (1.0 + lax.erf(a * (2.0 ** -0.5)))


def _expert_weights(n1_ref, e1_ref, r2_ref, e2_ref, at_ref, hw_ref, tc, il):
    ls = slice(tc * LANES, (tc + 1) * LANES)
    gates = [None] * (PEER_KEYS // PACK)
    for h in range(PEER_HEADS):
        nb = jnp.broadcast_to(n1_ref[h, il:il + 1, ls], (PACK, LANES)).astype(BF16)
        eb = jnp.broadcast_to(e1_ref[h, il:il + 1, ls], (PACK, LANES)).astype(BF16)
        for k in range(PEER_KEYS // PACK):
            ws = slice(PACK // 2 * k, PACK // 2 * (k + 1))
            e2 = _unpack_words(e2_ref[h, ws, ls])
            keep = _unpack_words(r2_ref[h, ws, ls]) < nb
            if h == 0:
                gates[k] = jnp.where(keep, e2 * eb, jnp.zeros_like(e2))
            else:
                gates[k] = jnp.where(keep, gates[k] + e2 * eb, gates[k])
    for k in range(PEER_KEYS // PACK):
        e0 = il * PEER_KEYS + PACK * k
        act = _gelu(at_ref[e0:e0 + PACK, ls]).astype(BF16)
        hw_ref[e0 // 2:(e0 + PACK) // 2, ls] = pltpu.bitcast(gates[k] * act, jnp.uint32)


MXU_COLS = 256


def _peer_kernel(h2_ref, u_ref, vt_ref,
                 n1p_ref, e1p_ref, r2p_ref, e2p_ref, n1c_ref, e1c_ref, r2c_ref, e2c_ref,
                 x1_ref, g2_ref, y_ref, at0_scr, at1_scr, hw0_scr, hw1_scr, acc_scr,
                 *, pairs_per_block):
    g = pl.program_id(0)
    ec = PEER_CHUNK
    rows = x1_ref.shape[0]

    @pl.when(g == 0)
    def _():
        at1_scr[...] = jnp.zeros_like(at1_scr)
        hw0_scr[...] = jnp.zeros_like(hw0_scr)
        hw1_scr[...] = jnp.zeros_like(hw1_scr)
        acc_scr[...] = jnp.zeros_like(acc_scr)

    pair = jnp.maximum(g - 1, 0) % pairs_per_block

    def half_step(uw_row0, routing, at_w, at_r, hw_w, hw_r, vtw_col0, restart):
        for n in range(rows // MXU_COLS):
            ts = slice(n * MXU_COLS, (n + 1) * MXU_COLS)
            tw = slice(n * MXU_COLS // 2, (n + 1) * MXU_COLS // 2)
            at_w[:, ts] = _dot_nt(_unpack_words(u_ref[uw_row0:uw_row0 + ec // 2, :]),
                                  _unpack_words(h2_ref[tw, :]))
            for tc in range(n * (MXU_COLS // LANES), (n + 1) * (MXU_COLS // LANES)):
                for il in range(ec // PEER_KEYS):
                    _expert_weights(*routing, at_r, hw_w, tc, il)
            acc = acc_scr[:, ts]
            if restart is not None:
                acc = jnp.where(restart, 0.0, acc)
            acc_scr[:, ts] = acc + _dot(_unpack_words(vt_ref[:, vtw_col0:vtw_col0 + ec]),
                                        _unpack_words(hw_r[:, ts]))

    half_step(0, (n1p_ref, e1p_ref, r2p_ref, e2p_ref), at0_scr, at1_scr,
              hw1_scr, hw0_scr, 0, pair == 0)
    half_step(ec // 2, (n1c_ref, e1c_ref, r2c_ref, e2c_ref), at1_scr, at0_scr,
              hw0_scr, hw1_scr, ec, None)

    @pl.when((g > 0) & (pair == pairs_per_block - 1))
    def _():
        y_ref[...] = x1_ref[...] + g2_ref[...] * acc_scr[...].T


def _peer(h2, u, vt, n1, e1, r2, e2, x1, g2):
    t = x1.shape[0]
    rows = min(PEER_ROWS, t)
    n_chunks = PEER_EXPERTS // PEER_CHUNK
    ppb = n_chunks // 2
    n_pairs = (t // rows) * ppb
    n_items = 2 * n_pairs
    il_rows = PEER_CHUNK // PEER_KEYS

    pair_a = lambda g: jnp.minimum(g, n_pairs - 1)
    item_p = lambda g: jnp.maximum(2 * g - 1, 0)
    item_c = lambda g: jnp.minimum(2 * g, n_items - 1)
    block_c = lambda g: jnp.maximum(g - 1, 0) // ppb

    def routing_specs(item):
        per_chunk = pl.BlockSpec((PEER_HEADS, il_rows, rows),
                                 lambda g: (0, item(g) % n_chunks, item(g) // n_chunks))
        per_block = pl.BlockSpec((PEER_HEADS, PEER_KEYS // 2, rows),
                                 lambda g: (0, 0, item(g) // n_chunks))
        return [per_chunk, per_chunk, per_block, per_block]

    if g2.shape[1] == 1:
        blocks_per_seq = (t // g2.shape[0]) // rows
        g2_spec = pl.BlockSpec((None, 1, D_MODEL), lambda g: (block_c(g) // blocks_per_seq, 0, 0))
    else:
        g2_spec = pl.BlockSpec((None, rows, D_MODEL), lambda g: (0, block_c(g), 0))
    tok_c = pl.BlockSpec((rows, D_MODEL), lambda g: (block_c(g), 0))
    return pl.pallas_call(
        functools.partial(_peer_kernel, pairs_per_block=ppb),
        out_shape=jax.ShapeDtypeStruct((t, D_MODEL), F32),
        grid=(n_pairs + 1,),
        in_specs=[
            pl.BlockSpec((rows // 2, D_MODEL), lambda g: (pair_a(g) // ppb, 0)),
            pl.BlockSpec((PEER_CHUNK, D_MODEL), lambda g: (pair_a(g) % ppb, 0)),
            pl.BlockSpec((D_MODEL // 2, 2 * PEER_CHUNK), lambda g: (0, jnp.maximum(g - 1, 0) % ppb)),
            *routing_specs(item_p), *routing_specs(item_c),
            tok_c, g2_spec,
        ],
        out_specs=tok_c,
        scratch_shapes=[pltpu.VMEM((PEER_CHUNK, rows), F32),
                        pltpu.VMEM((PEER_CHUNK, rows), F32),
                        pltpu.VMEM((PEER_CHUNK // 2, rows), jnp.uint32),
                        pltpu.VMEM((PEER_CHUNK // 2, rows), jnp.uint32),
                        pltpu.VMEM((D_MODEL, rows), F32)],
        compiler_params=_params("arbitrary"),
        name="peer_experts",
    )(h2, u, vt, n1, e1, r2, e2, n1, e1, r2, e2, x1, g2)


def _dup_heads(w):
    lead = w.shape[:-1]
    w = w.reshape(lead + (N_KV_HEADS, 1, HEAD_DIM))
    return jnp.broadcast_to(w, lead + (N_KV_HEADS, 2, HEAD_DIM)).reshape(lead + (2 * KV_W,))


def _first_of_dup(a):
    return a.reshape(a.shape[:-1] + (N_KV_HEADS, 2, HEAD_DIM))[..., 0, :]


def kernel(x_prompt, x_sample, cache_k, cache_v, state_conv, c_prompt, c_sample, w_ada, b_ada, norm1_w, w_in, q_norm_w, k_norm_w, sinks, conv_w, conv_bias, w_out, norm2_w, w_query, sub_keys, expert_u, expert_v):
    depth = w_ada.shape[0]
    batch, seq, _ = x_prompt.shape
    nseq, dec = x_sample.shape[:2]
    assert dec <= SAMPLE_Q_ROWS and dec >= CONV_W - 1
    ts = nseq * dec

    head_of_col = jnp.arange(D_MODEL, dtype=jnp.int32) // HEAD_DIM
    hsum = (head_of_col[:, None] == jnp.arange(LANES, dtype=jnp.int32)[None, :]).astype(BF16)
    hexp = hsum.T
    c_all = jnp.concatenate([c_prompt, c_sample], axis=0)
    c_all = jnp.pad(c_all, ((0, (-c_all.shape[0]) % 8), (0, 0)))

    yp = x_prompt
    ys = x_sample.reshape(1, ts, D_MODEL)
    kp, vp, cp, ksm, vsm, csm = [], [], [], [], [], []
    for l in range(depth):
        win = w_in[l]
        win2 = jnp.concatenate(
            [win[:, :D_MODEL], _dup_heads(win[:, D_MODEL:D_MODEL + KV_W]),
             _dup_heads(win[:, D_MODEL + KV_W:D_MODEL + 2 * KV_W]), win[:, D_MODEL + 2 * KV_W:]],
            axis=1).astype(BF16)
        nw1 = norm1_w[l].reshape(1, D_MODEL)
        nw2 = norm2_w[l].reshape(1, D_MODEL)
        qw = jnp.tile(q_norm_w[l], N_HEADS).reshape(1, D_MODEL)
        kw = jnp.tile(k_norm_w[l], 2 * N_KV_HEADS).reshape(1, 2 * KV_W)
        cw = conv_w[l]
        cb = conv_bias[l].reshape(1, D_MODEL)
        wout = w_out[l].astype(BF16)
        wqt = w_query[l].T.astype(BF16)
        keys = sub_keys[l].reshape(2 * PEER_HEADS, PEER_KEYS, PEER_HALF)
        u = _pack_table(expert_u[l])
        vt = _pack_table_transposed(expert_v[l])

        mod = _adaln(c_all, w_ada[l], b_ada[l])
        mod_p = [mod[j, :batch].reshape(batch, 1, D_MODEL) for j in range(6)]
        mod_s = [jnp.repeat(mod[j, batch:batch + nseq], dec, axis=0).reshape(1, ts, D_MODEL)
                 for j in range(6)]

        q, k2, v2, cu, gc, sa = _project(yp, mod_p[1], mod_p[0], nw1, win2, qw, kw, hsum, hexp)
        x1, h2 = _prompt_mix(sinks[l], yp, q, k2, v2, cu, gc, sa, mod_p[2], mod_p[4], mod_p[3],
                             cw, cb, wout, nw2)
        h2 = h2.reshape(batch * seq // 2, D_MODEL)
        n1, e1, r2, e2 = _route(h2, wqt, keys)
        yp = _peer(h2, u, vt, n1, e1, r2, e2, x1.reshape(batch * seq, D_MODEL), mod_p[5])
        yp = yp.reshape(batch, seq, D_MODEL)
        kp.append(_first_of_dup(k2[:, seq - WINDOW:]))
        vp.append(_first_of_dup(v2[:, seq - WINDOW:]))
        cp.append(cu[:, seq - (CONV_W - 1):])

        q, k2, v2, cu, gc, sa = _project(ys, mod_s[1], mod_s[0], nw1, win2, qw, kw, hsum, hexp)
        pad_rows = lambda a: jnp.pad(a.reshape(nseq, dec, a.shape[-1]),
                                     ((0, 0), (0, SAMPLE_Q_ROWS - dec), (0, 0)))
        oat = _sample_attn(sinks[l], pad_rows(q[0]),
                           _dup_heads(cache_k[l].reshape(nseq, WINDOW, KV_W)).astype(BF16),
                           pad_rows(k2[0]).astype(BF16),
                           _dup_heads(cache_v[l].reshape(nseq, WINDOW, KV_W)).astype(BF16),
                           pad_rows(v2[0]).astype(BF16))
        oat = oat[:, :dec].reshape(ts, D_MODEL)
        cu_seq = cu.reshape(nseq, dec, D_MODEL)
        up = jnp.concatenate([state_conv[l], cu_seq], axis=1)
        prev2 = up[:, 0:dec].reshape(ts, D_MODEL)
        prev1 = up[:, 1:dec + 1].reshape(ts, D_MODEL)
        flat = lambda a: a.reshape(ts, D_MODEL)
        x1, h2 = _sample_mix(flat(ys), flat(cu), prev1, prev2, flat(gc), flat(sa), oat,
                             flat(mod_s[2]), flat(mod_s[4]), flat(mod_s[3]), cw, cb, wout, nw2)
        n1, e1, r2, e2 = _route(h2, wqt, keys)
        ys = _peer(h2, u, vt, n1, e1, r2, e2, x1, mod_s[5]).reshape(1, ts, D_MODEL)
        k_new = _first_of_dup(k2[0]).reshape(nseq, dec, N_KV_HEADS, HEAD_DIM)
        v_new = _first_of_dup(v2[0]).reshape(nseq, dec, N_KV_HEADS, HEAD_DIM)
        ksm.append(jnp.concatenate([cache_k[l], k_new], axis=1)[:, -WINDOW:])
        vsm.append(jnp.concatenate([cache_v[l], v_new], axis=1)[:, -WINDOW:])
        csm.append(up[:, -(CONV_W - 1):])

    return (yp, ys.reshape(nseq, dec, D_MODEL), jnp.stack(kp), jnp.stack(vp), jnp.stack(cp),
            jnp.stack(ksm), jnp.stack(vsm), jnp.stack(csm))
```

```python
import functools

import jax
import jax.numpy as jnp
from jax import lax
from jax.experimental import pallas as pl
from jax.experimental.pallas import tpu as pltpu

F32 = jnp.float32
BF16 = jnp.bfloat16

D_MODEL = 1024
N_HEADS = 16
N_KV_HEADS = 4
HEAD_DIM = 64
WINDOW = 128
KV_W = N_KV_HEADS * HEAD_DIM
CONV_W = 3
PEER_HEADS = 8
PEER_KEYS = 128
PEER_TOPK = 16
PEER_HALF = 64
PEER_EXPERTS = PEER_KEYS * PEER_KEYS
RMS_EPS = 1e-6

LANES = 128
VMEM_LIMIT = 56 * 1024 * 1024

PROJ_ROWS = 512
ATTN_ROWS = 512
PACK_ROWS = 1024
ROUTE_ROWS = 512
ROUTE_HEADS = 8
PEER_ROWS = 512
PEER_CHUNK = 1024
SAMPLE_Q_ROWS = 8
SAMPLE_SEQ_BLOCK = 8

C_Q = 0
C_K = C_Q + D_MODEL
C_V = C_K + 2 * KV_W
C_U = C_V + 2 * KV_W
C_GPOST = C_U + D_MODEL
C_GPRE = C_GPOST + D_MODEL
C_GATTN = C_GPRE + D_MODEL
C_GCONV = C_GATTN + D_MODEL
C_END = C_GCONV + D_MODEL


def _params(*semantics):
    return pltpu.CompilerParams(dimension_semantics=semantics, vmem_limit_bytes=VMEM_LIMIT)


def _const_spec(shape):
    nd = len(shape)
    return pl.BlockSpec(shape, lambda *_: (0,) * nd, pipeline_mode=pl.Buffered(1))


def _mod_spec(arr, rows):
    if arr.shape[1] == 1:
        return pl.BlockSpec((None, 1, D_MODEL), lambda b, i, *_: (b, 0, 0))
    return pl.BlockSpec((None, rows, D_MODEL), lambda b, i, *_: (b, i, 0))


def _dot(a, b):
    return jnp.dot(a, b, preferred_element_type=F32)


def _dot_nt(a, b):
    return lax.dot_general(a, b, (((1,), (1,)), ((), ())), preferred_element_type=F32)


def _pack_words(x):
    return pltpu.bitcast(x.astype(BF16), jnp.uint32)


def _unpack_words(w):
    return pltpu.bitcast(w, BF16)


def _dot_split(a, b):
    hi = a.astype(BF16)
    lo = (a - hi.astype(F32)).astype(BF16)
    return _dot(hi, b) + _dot(lo, b)


def _ada_kernel(c_ref, w_ref, b_ref, o_ref):
    s = jax.nn.silu(c_ref[...])
    o_ref[...] = jnp.dot(s, w_ref[...], precision=lax.Precision.HIGHEST,
                         preferred_element_type=F32) + b_ref[...]


def _adaln(c_all, w_ada, b_ada):
    rows = c_all.shape[0]
    return pl.pallas_call(
        _ada_kernel,
        out_shape=jax.ShapeDtypeStruct((6, rows, D_MODEL), F32),
        grid=(6,),
        in_specs=[
            pl.BlockSpec((rows, D_MODEL), lambda j: (0, 0)),
            pl.BlockSpec((D_MODEL, D_MODEL), lambda j: (0, j)),
            pl.BlockSpec((1, D_MODEL), lambda j: (0, j)),
        ],
        out_specs=pl.BlockSpec((None, rows, D_MODEL), lambda j: (j, 0, 0)),
        compiler_params=_params("arbitrary"),
        name="adaln",
    )(c_all, w_ada, b_ada.reshape(1, 6 * D_MODEL))


def _pack_kernel(x_ref, o_ref):
    o_ref[...] = _pack_words(x_ref[...])


def _pack_t_kernel(x_ref, o_ref):
    o_ref[...] = _pack_words(x_ref[...].T)


def _pack_table(x):
    n, m = x.shape
    return pl.pallas_call(
        _pack_kernel,
        out_shape=jax.ShapeDtypeStruct((n // 2, m), jnp.uint32),
        grid=(n // PACK_ROWS,),
        in_specs=[pl.BlockSpec((PACK_ROWS, m), lambda i: (i, 0))],
        out_specs=pl.BlockSpec((PACK_ROWS // 2, m), lambda i: (i, 0)),
        compiler_params=_params("parallel"),
        name="pack_table",
    )(x)


def _pack_table_transposed(x):
    n, m = x.shape
    return pl.pallas_call(
        _pack_t_kernel,
        out_shape=jax.ShapeDtypeStruct((m // 2, n), jnp.uint32),
        grid=(n // PACK_ROWS,),
        in_specs=[pl.BlockSpec((PACK_ROWS, m), lambda i: (i, 0))],
        out_specs=pl.BlockSpec((m // 2, PACK_ROWS), lambda i: (0, i)),
        compiler_params=_params("parallel"),
        name="pack_table_t",
    )(x)


def _head_rms(z, hsum, hexp, w):
    ss = _dot((z * z).astype(BF16), hsum)
    r = lax.rsqrt(ss * (1.0 / HEAD_DIM) + RMS_EPS)
    return z * _dot_split(r, hexp) * w


def _proj_kernel(x_ref, sc_ref, sh_ref, nw_ref, win_ref, qw_ref, kw_ref, hsum_ref, hexp_ref,
                 q_ref, k2_ref, v2_ref, cu_ref, gc_ref, sa_ref):
    x = x_ref[...]
    h = x * lax.rsqrt(jnp.mean(x * x, axis=-1, keepdims=True) + RMS_EPS) * nw_ref[...]
    h = h * (1.0 + sc_ref[...]) + sh_ref[...]
    hb = h.astype(BF16)

    def seg(lo, hi):
        return _dot(hb, win_ref[:, lo:hi])

    q = _head_rms(seg(C_Q, C_K), hsum_ref[...], hexp_ref[...], qw_ref[...])
    q_ref[...] = (q * (HEAD_DIM ** -0.5)).astype(BF16)
    nk = 2 * KV_W
    k2_ref[...] = _head_rms(seg(C_K, C_V), hsum_ref[:nk, :], hexp_ref[:, :nk], kw_ref[...])
    v2_ref[...] = seg(C_V, C_U)
    cu_ref[...] = seg(C_GPRE, C_GATTN) * seg(C_U, C_GPOST)
    gc_ref[...] = (seg(C_GPOST, C_GPRE) * jax.nn.sigmoid(seg(C_GCONV, C_END))).astype(BF16)
    sa_ref[...] = jax.nn.sigmoid(seg(C_GATTN, C_GCONV)).astype(BF16)


def _project(x, sc1, sh1, nw, win2, qw, kw, hsum, hexp):
    nb, s, _ = x.shape
    rows = min(PROJ_ROWS, s)
    tok = lambda w, dt: jax.ShapeDtypeStruct((nb, s, w), dt)
    tspec = lambda w: pl.BlockSpec((None, rows, w), lambda b, i: (b, i, 0))
    return pl.pallas_call(
        _proj_kernel,
        out_shape=(tok(D_MODEL, BF16), tok(2 * KV_W, F32), tok(2 * KV_W, F32),
                   tok(D_MODEL, F32), tok(D_MODEL, BF16), tok(D_MODEL, BF16)),
        grid=(nb, s // rows),
        in_specs=[
            tspec(D_MODEL), _mod_spec(sc1, rows), _mod_spec(sh1, rows),
            _const_spec((1, D_MODEL)), _const_spec((D_MODEL, C_END)),
            _const_spec((1, D_MODEL)), _const_spec((1, 2 * KV_W)),
            _const_spec((D_MODEL, LANES)), _const_spec((LANES, D_MODEL)),
        ],
        out_specs=(tspec(D_MODEL), tspec(2 * KV_W), tspec(2 * KV_W),
                   tspec(D_MODEL), tspec(D_MODEL), tspec(D_MODEL)),
        compiler_params=_params("parallel", "parallel"),
        name="in_proj",
    )(x, sc1, sh1, nw, win2, qw, kw, hsum, hexp)


def _attend_blocks(blocks, sinks_ref):
    qb = blocks[0][0].shape[0]
    lane = lax.broadcasted_iota(jnp.int32, (qb, LANES), 1)
    low = lane < HEAD_DIM
    row = lax.broadcasted_iota(jnp.int32, (4 * qb, 2 * WINDOW), 0)
    col = lax.broadcasted_iota(jnp.int32, (4 * qb, 2 * WINDOW), 1)
    dist = col - (row & (qb - 1))
    band = (dist >= 1) & (dist <= WINDOW)
    blk = lax.broadcasted_iota(jnp.int32, (4 * qb, 1), 0) // qb
    ones = jnp.ones((2 * WINDOW, LANES), BF16)
    sinks = []
    for j in range(N_KV_HEADS):
        sink = jnp.full((4 * qb, 1), sinks_ref[4 * j + 3], F32)
        for g in range(2, -1, -1):
            sink = jnp.where(blk == g, sinks_ref[4 * j + g], sink)
        sinks.append(sink)

    scores = []
    for qblk, k2, _, _ in blocks:
        for j in range(N_KV_HEADS):
            qa = qblk[:, 2 * j * LANES:(2 * j + 1) * LANES].astype(F32)
            qc = qblk[:, (2 * j + 1) * LANES:(2 * j + 2) * LANES].astype(F32)
            lhs = jnp.concatenate([jnp.where(low, qa, 0.0), jnp.where(low, 0.0, qa),
                                   jnp.where(low, qc, 0.0), jnp.where(low, 0.0, qc)], axis=0)
            scores.append(_dot_nt(lhs.astype(BF16), k2[:, j * LANES:(j + 1) * LANES]))

    probs = []
    for b, (_, _, _, first_col) in enumerate(blocks):
        mask = band if isinstance(first_col, int) and first_col == 0 else band & (col >= first_col)
        for j in range(N_KV_HEADS):
            s = jnp.where(mask, scores[b * N_KV_HEADS + j], -jnp.inf)
            m = jnp.maximum(jnp.max(s, axis=-1, keepdims=True), sinks[j])
            probs.append((jnp.exp(s - m).astype(BF16), jnp.exp(sinks[j] - m)))

    results = []
    for b, (_, _, v2, _) in enumerate(blocks):
        outs = []
        for j in range(N_KV_HEADS):
            p, sink_term = probs[b * N_KV_HEADS + j]
            o = _dot(p, jnp.concatenate([v2[:, j * LANES:(j + 1) * LANES], ones], axis=1))
            on = o[:, :LANES] / (o[:, LANES:] + sink_term)
            outs.append(jnp.where(low, on[0:qb], on[qb:2 * qb]))
            outs.append(jnp.where(low, on[2 * qb:3 * qb], on[3 * qb:4 * qb]))
        results.append(jnp.concatenate(outs, axis=1))
    return results


def _merge_tail(x, cu, prev1, prev2, gc, sa, oattn, g1, sc2, sh2, cw_ref, cb, wout, n2w):
    conv_y = prev2 * cw_ref[0:1, :] + prev1 * cw_ref[1:2, :] + cu * cw_ref[2:3, :] + cb
    merged = sa.astype(F32) * oattn + gc.astype(F32) * conv_y
    x1 = x + g1 * _dot(merged.astype(BF16), wout)
    h2 = x1 * lax.rsqrt(jnp.mean(x1 * x1, axis=-1, keepdims=True) + RMS_EPS) * n2w
    h2 = h2 * (1.0 + sc2) + sh2
    return x1, _pack_words(h2)


def _prompt_mix_kernel(sinks_ref, x_ref, q_ref, k_ref, kp_ref, v_ref, vp_ref, cu_ref, cup_ref,
                       gc_ref, sa_ref, g1_ref, sc2_ref, sh2_ref, cw_ref, cb_ref, wout_ref, n2w_ref,
                       x1_ref, h2_ref, oat_scr):
    i = pl.program_id(1)
    rows = x_ref.shape[0]
    kext = jnp.concatenate([kp_ref[...], k_ref[...]], axis=0).astype(BF16)
    vext = jnp.concatenate([vp_ref[...], v_ref[...]], axis=0).astype(BF16)
    for w in range(rows // WINDOW):
        first_col = jnp.where(i == 0, WINDOW, 0) if w == 0 else 0
        lo = w * WINDOW
        oat_scr[lo:lo + WINDOW, :] = _attend_blocks(
            [(q_ref[lo:lo + WINDOW, :], kext[lo:lo + 2 * WINDOW, :], vext[lo:lo + 2 * WINDOW, :],
              first_col)], sinks_ref)[0]

    cu = cu_ref[...]
    has_prev = (i > 0).astype(F32)
    p1 = cup_ref[7:8, :] * has_prev
    p2 = cup_ref[6:7, :] * has_prev
    r = lax.broadcasted_iota(jnp.int32, (rows, 1), 0)
    prev1 = jnp.where(r == 0, p1, pltpu.roll(cu, 1, axis=0))
    prev2 = jnp.where(r == 0, p2, jnp.where(r == 1, p1, pltpu.roll(cu, 2, axis=0)))
    x1, h2 = _merge_tail(x_ref[...], cu, prev1, prev2, gc_ref[...], sa_ref[...], oat_scr[...],
                         g1_ref[...], sc2_ref[...], sh2_ref[...], cw_ref, cb_ref[...],
                         wout_ref[...], n2w_ref[...])
    x1_ref[...] = x1
    h2_ref[...] = h2


def _prompt_mix(sinks, x, q, k2, v2, cu, gc, sa, g1, sc2, sh2, cw, cb, wout, n2w):
    nb, s, _ = x.shape
    rows = ATTN_ROWS
    wpt = rows // WINDOW
    tspec = lambda w: pl.BlockSpec((None, rows, w), lambda b, i: (b, i, 0))
    prev_win = pl.BlockSpec((None, WINDOW, 2 * KV_W), lambda b, i: (b, jnp.maximum(i * wpt - 1, 0), 0))
    prev_rows = pl.BlockSpec((None, 8, D_MODEL), lambda b, i: (b, jnp.maximum(i * (rows // 8) - 1, 0), 0))
    mod = pl.BlockSpec((None, 1, D_MODEL), lambda b, i: (b, 0, 0))
    return pl.pallas_call(
        _prompt_mix_kernel,
        out_shape=(jax.ShapeDtypeStruct((nb, s, D_MODEL), F32),
                   jax.ShapeDtypeStruct((nb, s // 2, D_MODEL), jnp.uint32)),
        grid=(nb, s // rows),
        in_specs=[
            pl.BlockSpec(memory_space=pltpu.SMEM),
            tspec(D_MODEL), tspec(D_MODEL),
            tspec(2 * KV_W), prev_win, tspec(2 * KV_W), prev_win,
            tspec(D_MODEL), prev_rows, tspec(D_MODEL), tspec(D_MODEL),
            mod, mod, mod,
            _const_spec((CONV_W, D_MODEL)), _const_spec((1, D_MODEL)),
            _const_spec((D_MODEL, D_MODEL)), _const_spec((1, D_MODEL)),
        ],
        out_specs=(tspec(D_MODEL), pl.BlockSpec((None, rows // 2, D_MODEL), lambda b, i: (b, i, 0))),
        scratch_shapes=[pltpu.VMEM((rows, D_MODEL), F32)],
        compiler_params=_params("parallel", "parallel"),
        name="prompt_mix",
    )(sinks, x, q, k2, k2, v2, v2, cu, cu, gc, sa, g1, sc2, sh2, cw, cb, wout, n2w)


def _sample_attn_kernel(sinks_ref, q_ref, ck_ref, kn_ref, cv_ref, vn_ref, o_ref):
    pad = jnp.zeros((WINDOW - SAMPLE_Q_ROWS, 2 * KV_W), BF16)
    blocks = []
    for n in range(q_ref.shape[0]):
        kext = jnp.concatenate([ck_ref[n], kn_ref[n], pad], axis=0)
        vext = jnp.concatenate([cv_ref[n], vn_ref[n], pad], axis=0)
        blocks.append((q_ref[n], kext, vext, 0))
    for n, o in enumerate(_attend_blocks(blocks, sinks_ref)):
        o_ref[n] = o


def _sample_attn(sinks, q8, ck2, k2n, cv2, v2n):
    nseq = q8.shape[0]
    sb = SAMPLE_SEQ_BLOCK
    spec = lambda r, w: pl.BlockSpec((sb, r, w), lambda n: (n, 0, 0))
    return pl.pallas_call(
        _sample_attn_kernel,
        out_shape=jax.ShapeDtypeStruct((nseq, SAMPLE_Q_ROWS, D_MODEL), F32),
        grid=(nseq // sb,),
        in_specs=[
            pl.BlockSpec(memory_space=pltpu.SMEM),
            spec(SAMPLE_Q_ROWS, D_MODEL),
            spec(WINDOW, 2 * KV_W), spec(SAMPLE_Q_ROWS, 2 * KV_W),
            spec(WINDOW, 2 * KV_W), spec(SAMPLE_Q_ROWS, 2 * KV_W),
        ],
        out_specs=spec(SAMPLE_Q_ROWS, D_MODEL),
        compiler_params=_params("parallel"),
        name="sample_attn",
    )(sinks, q8, ck2, k2n, cv2, v2n)


def _sample_mix_kernel(x_ref, cu_ref, p1_ref, p2_ref, gc_ref, sa_ref, oat_ref,
                       g1_ref, sc2_ref, sh2_ref, cw_ref, cb_ref, wout_ref, n2w_ref,
                       x1_ref, h2_ref):
    x1, h2 = _merge_tail(x_ref[...], cu_ref[...], p1_ref[...], p2_ref[...], gc_ref[...],
                         sa_ref[...], oat_ref[...], g1_ref[...], sc2_ref[...], sh2_ref[...],
                         cw_ref, cb_ref[...], wout_ref[...], n2w_ref[...])
    x1_ref[...] = x1
    h2_ref[...] = h2


def _sample_mix(x, cu, prev1, prev2, gc, sa, oattn, g1, sc2, sh2, cw, cb, wout, n2w):
    t = x.shape[0]
    full = lambda: pl.BlockSpec((t, D_MODEL), lambda: (0, 0))
    return pl.pallas_call(
        _sample_mix_kernel,
        out_shape=(jax.ShapeDtypeStruct((t, D_MODEL), F32),
                   jax.ShapeDtypeStruct((t // 2, D_MODEL), jnp.uint32)),
        in_specs=[full() for _ in range(10)] + [
            pl.BlockSpec((CONV_W, D_MODEL), lambda: (0, 0)),
            pl.BlockSpec((1, D_MODEL), lambda: (0, 0)),
            pl.BlockSpec((D_MODEL, D_MODEL), lambda: (0, 0)),
            pl.BlockSpec((1, D_MODEL), lambda: (0, 0)),
        ],
        out_specs=(full(), pl.BlockSpec((t // 2, D_MODEL), lambda: (0, 0))),
        compiler_params=pltpu.CompilerParams(vmem_limit_bytes=VMEM_LIMIT),
        name="sample_mix",
    )(x, cu, prev1, prev2, gc, sa, oattn, g1, sc2, sh2, cw, cb, wout, n2w)


def _top16(s, index):
    rank = jnp.full(s.shape, float(PEER_TOPK), F32)
    vals = []
    for r in range(PEER_TOPK):
        m = jnp.max(s, axis=0, keepdims=True)
        first = jnp.min(jnp.where(s == m, index, float(s.shape[0])), axis=0, keepdims=True)
        hit = index == first
        rank = jnp.where(hit, float(r), rank)
        s = jnp.where(hit, -jnp.inf, s)
        vals.append(m)
    return vals, rank


def _row_array(rows, index):
    out = jnp.zeros(index.shape, F32)
    for r, v in enumerate(rows):
        out = jnp.where(index == float(r), v, out)
    return out


def _gates(s1, s2, v1_0, v2_0, z):
    return jnp.exp(s1 - v1_0) * (0.5 / z), jnp.exp(s2 - v2_0)


def _route_exact(s1, s2, idx16, idx128, idx256):
    v1, rank1 = _top16(s1, idx128)
    v2, rank2 = _top16(s2, idx128)
    v2a = _row_array(v2, idx16)
    cand = jnp.concatenate([v1[r] + v2a for r in range(PEER_TOPK)], axis=0)
    sel = jnp.zeros(cand.shape, F32)
    c = cand
    for _ in range(PEER_TOPK):
        m = jnp.max(c, axis=0, keepdims=True)
        first = jnp.min(jnp.where(c == m, idx256, float(PEER_TOPK * PEER_TOPK)),
                        axis=0, keepdims=True)
        hit = idx256 == first
        sel = jnp.where(hit, 1.0, sel)
        c = jnp.where(hit, -jnp.inf, c)
    z = jnp.sum(sel * jnp.exp(cand - (v1[0] + v2[0])), axis=0, keepdims=True)
    n1 = jnp.zeros(rank1.shape, F32)
    for r in range(PEER_TOPK):
        n_r = jnp.sum(sel[r * PEER_TOPK:(r + 1) * PEER_TOPK], axis=0, keepdims=True)
        n1 = jnp.where(rank1 == float(r), n_r, n1)
    e1, e2 = _gates(s1, s2, v1[0], v2[0], z)
    return n1, e1, rank2, e2


SUBLANES = 8


def _batcher_pairs(n):
    pairs = []
    p = 1
    while p < n:
        k = p
        while k >= 1:
            for j in range(k % p, n - k, 2 * k):
                for i in range(min(k, n - j - k)):
                    if (i + j) // (2 * p) == (i + j + k) // (2 * p):
                        pairs.append((i + j, i + j + k))
            k //= 2
        p *= 2
    return pairs


def _sort_desc(vs):
    vs = list(vs)
    for i, j in _batcher_pairs(len(vs)):
        vs[i], vs[j] = jnp.maximum(vs[i], vs[j]), jnp.minimum(vs[i], vs[j])
    return vs


def _merge_top(a, b):
    n = len(a)
    c = [jnp.maximum(a[i], b[n - 1 - i]) for i in range(n)]
    d = n // 2
    while d >= 1:
        for i in range(n):
            if i & d == 0:
                c[i], c[i + d] = jnp.maximum(c[i], c[i + d]), jnp.minimum(c[i], c[i + d])
        d //= 2
    return c


def _merge_sublanes(w):
    for shift in (4, 2, 1):
        w = _merge_top(w, [pltpu.roll(x, shift, axis=0) for x in w])
    return w


def _tiles(s):
    return [s[SUBLANES * j:SUBLANES * (j + 1)] for j in range(s.shape[0] // SUBLANES)]


def _has_adjacent_tie(v):
    gap = v[0] - v[1]
    for r in range(1, len(v) - 1):
        gap = jnp.minimum(gap, v[r] - v[r + 1])
    return gap == 0.0


def _count_rows(flags):
    acc = flags[0]
    for f in flags[1:]:
        acc = acc + f
    return jnp.sum(acc, axis=0, keepdims=True)


def _route_distinct(s1, s2, sub):
    k = PEER_TOPK
    t1, t2 = _tiles(s1), _tiles(s2)
    v1 = _merge_sublanes(_sort_desc(t1))
    v2 = _merge_sublanes(_sort_desc(t2))

    pick = lambda v: jnp.where(sub < 2.0, v[0], jnp.where(sub < 4.0, v[1], jnp.where(sub < 6.0, v[2], v[3])))
    a_const, b_const = pick(v1), pick(v2)
    lengths = (16, 15, 7, 6, 3, 2, 1)
    lists = []
    for i in range(k):
        a, b = a_const, b_const
        for s_col, s_row, off in ((5.0, 4.0, 2), (3.0, 2.0, 1), (1.0, 0.0, 0)):
            if i < lengths[int(s_col)]:
                a = jnp.where(sub == s_col, v1[i + off + 1], a)
            if i < lengths[int(s_row)]:
                b = jnp.where(sub == s_row, v2[i + off], b)
        live = float(sum(1 for n in lengths if n > i))
        lists.append(jnp.where(sub < live, a + b, -jnp.inf))
    c = _merge_sublanes(lists)
    tau = c[k - 1]

    z = None
    for x in lists:
        e = jnp.where(x >= tau, jnp.exp(x - c[0]), 0.0)
        z = e if z is None else z + e
    z = jnp.sum(z, axis=0, keepdims=True)

    counts, total = [], None
    for r1 in range(k):
        n_r = jnp.zeros_like(tau)
        for r2 in range(k // (r1 + 1)):
            n_r = jnp.where(v1[r1] + v2[r2] >= tau, float(r2 + 1), n_r)
        counts.append(n_r)
        total = n_r if total is None else total + n_r

    n1, rank2, lead1, lead2 = [], [], [], []
    for x1, x2 in zip(t1, t2):
        n = jnp.zeros_like(x1)
        for r in range(k - 1, -1, -1):
            n = jnp.where(x1 >= v1[r], counts[r], n)
        rk = jnp.zeros_like(x2)
        for r in range(k):
            rk = jnp.where(v2[r] > x2, float(r + 1), rk)
        n1.append(n)
        rank2.append(rk)
        lead1.append(jnp.where(x1 >= v1[k - 1], 1.0, 0.0))
        lead2.append(jnp.where(x2 >= v2[k - 1], 1.0, 0.0))
    n1 = jnp.concatenate(n1, axis=0)
    rank2 = jnp.concatenate(rank2, axis=0)

    want = float(k)
    tie = _has_adjacent_tie(v1) | _has_adjacent_tie(v2) | _has_adjacent_tie(c)
    valid = (~tie) & (_count_rows(lead1) == want) & (_count_rows(lead2) == want) & (total == want)
    e1, e2 = _gates(s1, s2, v1[0][0:1], v2[0][0:1], z)
    return n1, e1, rank2, e2, valid


def _route_kernel(h2_ref, wqt_ref, keys_ref, n1_ref, e1_ref, r2_ref, e2_ref, qpt_scr):
    rows = 2 * h2_ref.shape[0]
    qpt_scr[...] = _dot_nt(wqt_ref[...], _unpack_words(h2_ref[...]))
    idx8 = lax.broadcasted_iota(jnp.int32, (8, LANES), 0).astype(F32)
    idx16 = lax.broadcasted_iota(jnp.int32, (PEER_TOPK, LANES), 0).astype(F32)

    groups = PEER_HEADS // ROUTE_HEADS

    def body(it, carry):
        tc = it // groups
        ls = pl.ds(pl.multiple_of(tc * LANES, LANES), LANES)
        hp = lax.Precision.HIGHEST
        heads, scores = [], []
        for j in range(ROUTE_HEADS):
            h = (it % groups) * ROUTE_HEADS + j
            base = pl.multiple_of(h * (2 * PEER_HALF), 2 * PEER_HALF)
            q1 = qpt_scr[pl.ds(base, PEER_HALF), ls]
            q2 = qpt_scr[pl.ds(base + PEER_HALF, PEER_HALF), ls]
            s1 = jnp.dot(keys_ref[2 * h], q1, precision=hp, preferred_element_type=F32)
            s2 = jnp.dot(keys_ref[2 * h + 1], q2, precision=hp, preferred_element_type=F32)
            heads.append(h)
            scores.append((s1, s2))

        def store(h, n1, e1, rank2, e2):
            n1_ref[h, :, ls] = n1
            e1_ref[h, :, ls] = e1
            r2_ref[h, :, ls] = _pack_words(rank2)
            e2_ref[h, :, ls] = _pack_words(e2)

        tied = []
        for h, (s1, s2) in zip(heads, scores):
            n1, e1, rank2, e2, valid = _route_distinct(s1, s2, idx8)
            store(h, n1, e1, rank2, e2)
            tied.append(jnp.max(jnp.where(valid, 0.0, 1.0)) > 0.0)

        for h, (s1, s2), redo in zip(heads, scores, tied):
            @pl.when(redo)
            def _():
                idx128 = lax.broadcasted_iota(jnp.int32, (PEER_KEYS, LANES), 0).astype(F32)
                idx256 = lax.broadcasted_iota(jnp.int32, (PEER_TOPK * PEER_TOPK, LANES), 0).astype(F32)
                store(h, *_route_exact(s1, s2, idx16, idx128, idx256))

        return carry

    lax.fori_loop(0, (rows // LANES) * groups, body, 0)


def _route(h2, wqt, keys):
    t = 2 * h2.shape[0]
    rows = min(ROUTE_ROWS, t)
    out = jax.ShapeDtypeStruct((PEER_HEADS, PEER_KEYS, t), F32)
    outw = jax.ShapeDtypeStruct((PEER_HEADS, PEER_KEYS // 2, t), jnp.uint32)
    ospec = pl.BlockSpec((PEER_HEADS, PEER_KEYS, rows), lambda i: (0, 0, i))
    wspec = pl.BlockSpec((PEER_HEADS, PEER_KEYS // 2, rows), lambda i: (0, 0, i))
    return pl.pallas_call(
        _route_kernel,
        out_shape=(out, out, outw, outw),
        grid=(t // rows,),
        in_specs=[
            pl.BlockSpec((rows // 2, D_MODEL), lambda i: (i, 0)),
            _const_spec((D_MODEL, D_MODEL)),
            _const_spec((2 * PEER_HEADS, PEER_KEYS, PEER_HALF)),
        ],
        out_specs=(ospec, ospec, wspec, wspec),
        scratch_shapes=[pltpu.VMEM((D_MODEL, rows), F32)],
        compiler_params=_params("parallel"),
        name="peer_route",
    )(h2, wqt, keys)


PACK = 16


def _gelu_doubled(a):
    return a * (1.0 + lax.erf(a * (2.0 ** -0.5)))


def _expert_weights(n1_ref, e1_ref, r2_ref, e2_ref, at_ref, hw_ref, tc, il):
    ls = slice(tc * LANES, (tc + 1) * LANES)
    gates = [None] * (PEER_KEYS // PACK)
    for h in range(PEER_HEADS):
        nb = jnp.broadcast_to(n1_ref[h, il:il + 1, ls], (PACK, LANES)).astype(BF16)
        eb = jnp.broadcast_to(e1_ref[h, il:il + 1, ls], (PACK, LANES)).astype(BF16)
        for k in range(PEER_KEYS // PACK):
            ws = slice(PACK // 2 * k, PACK // 2 * (k + 1))
            e2 = _unpack_words(e2_ref[h, ws, ls])
            keep = _unpack_words(r2_ref[h, ws, ls]) < nb
            if h == 0:
                gates[k] = jnp.where(keep, e2 * eb, jnp.zeros_like(e2))
            else:
                gates[k] = jnp.where(keep, gates[k] + e2 * eb, gates[k])
    for k in range(PEER_KEYS // PACK):
        e0 = il * PEER_KEYS + PACK * k
        act = _gelu_doubled(at_ref[e0:e0 + PACK, ls]).astype(BF16)
        hw_ref[e0 // 2:(e0 + PACK) // 2, ls] = pltpu.bitcast(gates[k] * act, jnp.uint32)


MXU_COLS = 256


def _peer_kernel(h2_ref, u_ref, vt_ref,
                 n1p_ref, e1p_ref, r2p_ref, e2p_ref, n1c_ref, e1c_ref, r2c_ref, e2c_ref,
                 x1_ref, g2_ref, y_ref, at0_scr, at1_scr, hw0_scr, hw1_scr, acc_scr,
                 *, pairs_per_block):
    g = pl.program_id(0)
    ec = PEER_CHUNK
    rows = x1_ref.shape[0]

    @pl.when(g == 0)
    def _():
        at1_scr[...] = jnp.zeros_like(at1_scr)
        hw0_scr[...] = jnp.zeros_like(hw0_scr)
        hw1_scr[...] = jnp.zeros_like(hw1_scr)
        acc_scr[...] = jnp.zeros_like(acc_scr)

    pair = jnp.maximum(g - 1, 0) % pairs_per_block

    def half_step(uw_row0, routing, at_w, at_r, hw_w, hw_r, vtw_col0, restart):
        for n in range(rows // MXU_COLS):
            ts = slice(n * MXU_COLS, (n + 1) * MXU_COLS)
            tw = slice(n * MXU_COLS // 2, (n + 1) * MXU_COLS // 2)
            at_w[:, ts] = _dot_nt(_unpack_words(u_ref[uw_row0:uw_row0 + ec // 2, :]),
                                  _unpack_words(h2_ref[tw, :]))
            for tc in range(n * (MXU_COLS // LANES), (n + 1) * (MXU_COLS // LANES)):
                for il in range(ec // PEER_KEYS):
                    _expert_weights(*routing, at_r, hw_w, tc, il)
            acc = acc_scr[:, ts]
            if restart is not None:
                acc = jnp.where(restart, 0.0, acc)
            acc_scr[:, ts] = acc + _dot(_unpack_words(vt_ref[:, vtw_col0:vtw_col0 + ec]),
                                        _unpack_words(hw_r[:, ts]))

    half_step(0, (n1p_ref, e1p_ref, r2p_ref, e2p_ref), at0_scr, at1_scr,
              hw1_scr, hw0_scr, 0, pair == 0)
    half_step(ec // 2, (n1c_ref, e1c_ref, r2c_ref, e2c_ref), at1_scr, at0_scr,
              hw0_scr, hw1_scr, ec, None)

    @pl.when((g > 0) & (pair == pairs_per_block - 1))
    def _():
        y_ref[...] = x1_ref[...] + g2_ref[...] * acc_scr[...].T


def _peer(h2, u, vt, n1, e1, r2, e2, x1, g2):
    t = x1.shape[0]
    rows = min(PEER_ROWS, t)
    n_chunks = PEER_EXPERTS // PEER_CHUNK
    ppb = n_chunks // 2
    n_pairs = (t // rows) * ppb
    n_items = 2 * n_pairs
    il_rows = PEER_CHUNK // PEER_KEYS

    pair_a = lambda g: jnp.minimum(g, n_pairs - 1)
    item_p = lambda g: jnp.maximum(2 * g - 1, 0)
    item_c = lambda g: jnp.minimum(2 * g, n_items - 1)
    block_c = lambda g: jnp.maximum(g - 1, 0) // ppb

    def routing_specs(item):
        per_chunk = pl.BlockSpec((PEER_HEADS, il_rows, rows),
                                 lambda g: (0, item(g) % n_chunks, item(g) // n_chunks))
        per_block = pl.BlockSpec((PEER_HEADS, PEER_KEYS // 2, rows),
                                 lambda g: (0, 0, item(g) // n_chunks))
        return [per_chunk, per_chunk, per_block, per_block]

    if g2.shape[1] == 1:
        blocks_per_seq = (t // g2.shape[0]) // rows
        g2_spec = pl.BlockSpec((None, 1, D_MODEL), lambda g: (block_c(g) // blocks_per_seq, 0, 0))
    else:
        g2_spec = pl.BlockSpec((None, rows, D_MODEL), lambda g: (0, block_c(g), 0))
    tok_c = pl.BlockSpec((rows, D_MODEL), lambda g: (block_c(g), 0))
    return pl.pallas_call(
        functools.partial(_peer_kernel, pairs_per_block=ppb),
        out_shape=jax.ShapeDtypeStruct((t, D_MODEL), F32),
        grid=(n_pairs + 1,),
        in_specs=[
            pl.BlockSpec((rows // 2, D_MODEL), lambda g: (pair_a(g) // ppb, 0)),
            pl.BlockSpec((PEER_CHUNK, D_MODEL), lambda g: (pair_a(g) % ppb, 0)),
            pl.BlockSpec((D_MODEL // 2, 2 * PEER_CHUNK), lambda g: (0, jnp.maximum(g - 1, 0) % ppb)),
            *routing_specs(item_p), *routing_specs(item_c),
            tok_c, g2_spec,
        ],
        out_specs=tok_c,
        scratch_shapes=[pltpu.VMEM((PEER_CHUNK, rows), F32),
                        pltpu.VMEM((PEER_CHUNK, rows), F32),
                        pltpu.VMEM((PEER_CHUNK // 2, rows), jnp.uint32),
                        pltpu.VMEM((PEER_CHUNK // 2, rows), jnp.uint32),
                        pltpu.VMEM((D_MODEL, rows), F32)],
        compiler_params=_params("arbitrary"),
        name="peer_experts",
    )(h2, u, vt, n1, e1, r2, e2, n1, e1, r2, e2, x1, g2)


def _dup_heads(w):
    lead = w.shape[:-1]
    w = w.reshape(lead + (N_KV_HEADS, 1, HEAD_DIM))
    return jnp.broadcast_to(w, lead + (N_KV_HEADS, 2, HEAD_DIM)).reshape(lead + (2 * KV_W,))


def _first_of_dup(a):
    return a.reshape(a.shape[:-1] + (N_KV_HEADS, 2, HEAD_DIM))[..., 0, :]


def kernel(x_prompt, x_sample, cache_k, cache_v, state_conv, c_prompt, c_sample, w_ada, b_ada, norm1_w, w_in, q_norm_w, k_norm_w, sinks, conv_w, conv_bias, w_out, norm2_w, w_query, sub_keys, expert_u, expert_v):
    depth = w_ada.shape[0]
    batch, seq, _ = x_prompt.shape
    nseq, dec = x_sample.shape[:2]
    assert dec <= SAMPLE_Q_ROWS and dec >= CONV_W - 1
    ts = nseq * dec

    head_of_col = jnp.arange(D_MODEL, dtype=jnp.int32) // HEAD_DIM
    hsum = (head_of_col[:, None] == jnp.arange(LANES, dtype=jnp.int32)[None, :]).astype(BF16)
    hexp = hsum.T
    c_all = jnp.concatenate([c_prompt, c_sample], axis=0)
    c_all = jnp.pad(c_all, ((0, (-c_all.shape[0]) % 8), (0, 0)))

    yp = x_prompt
    ys = x_sample.reshape(1, ts, D_MODEL)
    kp, vp, cp, ksm, vsm, csm = [], [], [], [], [], []
    for l in range(depth):
        win = w_in[l]
        win2 = jnp.concatenate(
            [win[:, :D_MODEL], _dup_heads(win[:, D_MODEL:D_MODEL + KV_W]),
             _dup_heads(win[:, D_MODEL + KV_W:D_MODEL + 2 * KV_W]), win[:, D_MODEL + 2 * KV_W:]],
            axis=1).astype(BF16)
        nw1 = norm1_w[l].reshape(1, D_MODEL)
        nw2 = norm2_w[l].reshape(1, D_MODEL)
        qw = jnp.tile(q_norm_w[l], N_HEADS).reshape(1, D_MODEL)
        kw = jnp.tile(k_norm_w[l], 2 * N_KV_HEADS).reshape(1, 2 * KV_W)
        cw = conv_w[l]
        cb = conv_bias[l].reshape(1, D_MODEL)
        wout = w_out[l].astype(BF16)
        wqt = w_query[l].T.astype(BF16)
        keys = sub_keys[l].reshape(2 * PEER_HEADS, PEER_KEYS, PEER_HALF)
        u = _pack_table(expert_u[l])
        vt = _pack_table_transposed(expert_v[l])

        mod = _adaln(c_all, w_ada[l], b_ada[l])
        mod_p = [mod[j, :batch].reshape(batch, 1, D_MODEL) for j in range(6)]
        mod_s = [jnp.repeat(mod[j, batch:batch + nseq], dec, axis=0).reshape(1, ts, D_MODEL)
                 for j in range(6)]

        q, k2, v2, cu, gc, sa = _project(yp, mod_p[1], mod_p[0], nw1, win2, qw, kw, hsum, hexp)
        x1, h2 = _prompt_mix(sinks[l], yp, q, k2, v2, cu, gc, sa, mod_p[2], mod_p[4], mod_p[3],
                             cw, cb, wout, nw2)
        h2 = h2.reshape(batch * seq // 2, D_MODEL)
        n1, e1, r2, e2 = _route(h2, wqt, keys)
        yp = _peer(h2, u, vt, n1, e1, r2, e2, x1.reshape(batch * seq, D_MODEL), mod_p[5])
        yp = yp.reshape(batch, seq, D_MODEL)
        kp.append(_first_of_dup(k2[:, seq - WINDOW:]))
        vp.append(_first_of_dup(v2[:, seq - WINDOW:]))
        cp.append(cu[:, seq - (CONV_W - 1):])

        q, k2, v2, cu, gc, sa = _project(ys, mod_s[1], mod_s[0], nw1, win2, qw, kw, hsum, hexp)
        pad_rows = lambda a: jnp.pad(a.reshape(nseq, dec, a.shape[-1]),
                                     ((0, 0), (0, SAMPLE_Q_ROWS - dec), (0, 0)))
        oat = _sample_attn(sinks[l], pad_rows(q[0]),
                           _dup_heads(cache_k[l].reshape(nseq, WINDOW, KV_W)).astype(BF16),
                           pad_rows(k2[0]).astype(BF16),
                           _dup_heads(cache_v[l].reshape(nseq, WINDOW, KV_W)).astype(BF16),
                           pad_rows(v2[0]).astype(BF16))
        oat = oat[:, :dec].reshape(ts, D_MODEL)
        cu_seq = cu.reshape(nseq, dec, D_MODEL)
        up = jnp.concatenate([state_conv[l], cu_seq], axis=1)
        prev2 = up[:, 0:dec].reshape(ts, D_MODEL)
        prev1 = up[:, 1:dec + 1].reshape(ts, D_MODEL)
        flat = lambda a: a.reshape(ts, D_MODEL)
        x1, h2 = _sample_mix(flat(ys), flat(cu), prev1, prev2, flat(gc), flat(sa), oat,
                             flat(mod_s[2]), flat(mod_s[4]), flat(mod_s[3]), cw, cb, wout, nw2)
        n1, e1, r2, e2 = _route(h2, wqt, keys)
        ys = _peer(h2, u, vt, n1, e1, r2, e2, x1, mod_s[5]).reshape(1, ts, D_MODEL)
        k_new = _first_of_dup(k2[0]).reshape(nseq, dec, N_KV_HEADS, HEAD_DIM)
        v_new = _first_of_dup(v2[0]).reshape(nseq, dec, N_KV_HEADS, HEAD_DIM)
        ksm.append(jnp.concatenate([cache_k[l], k_new], axis=1)[:, -WINDOW:])
        vsm.append(jnp.concatenate([cache_v[l], v_new], axis=1)[:, -WINDOW:])
        csm.append(up[:, -(CONV_W - 1):])

    return (yp, ys.reshape(nseq, dec, D_MODEL), jnp.stack(kp), jnp.stack(vp), jnp.stack(cp),
            jnp.stack(ksm), jnp.stack(vsm), jnp.stack(csm))
```

```python
import functools

import jax
import jax.numpy as jnp
from jax import lax
from jax.experimental import pallas as pl
from jax.experimental.pallas import tpu as pltpu

F32 = jnp.float32
BF16 = jnp.bfloat16

D_MODEL = 1024
N_HEADS = 16
N_KV_HEADS = 4
HEAD_DIM = 64
WINDOW = 128
KV_W = N_KV_HEADS * HEAD_DIM
CONV_W = 3
PEER_HEADS = 8
PEER_KEYS = 128
PEER_TOPK = 16
PEER_HALF = 64
PEER_EXPERTS = PEER_KEYS * PEER_KEYS
RMS_EPS = 1e-6

LANES = 128
VMEM_LIMIT = 56 * 1024 * 1024

PROJ_ROWS = 512
ATTN_ROWS = 512
PACK_ROWS = 1024
ROUTE_ROWS = 512
ROUTE_HEADS = 8
PEER_ROWS = 512
PEER_CHUNK = 1024
SAMPLE_Q_ROWS = 8
SAMPLE_SEQ_BLOCK = 8

C_Q = 0
C_K = C_Q + D_MODEL
C_V = C_K + 2 * KV_W
C_U = C_V + 2 * KV_W
C_GPOST = C_U + D_MODEL
C_GPRE = C_GPOST + D_MODEL
C_GATTN = C_GPRE + D_MODEL
C_GCONV = C_GATTN + D_MODEL
C_END = C_GCONV + D_MODEL


def _params(*semantics):
    return pltpu.CompilerParams(dimension_semantics=semantics, vmem_limit_bytes=VMEM_LIMIT)


def _const_spec(shape):
    nd = len(shape)
    return pl.BlockSpec(shape, lambda *_: (0,) * nd, pipeline_mode=pl.Buffered(1))


def _mod_spec(arr, rows):
    if arr.shape[1] == 1:
        return pl.BlockSpec((None, 1, D_MODEL), lambda b, i, *_: (b, 0, 0))
    return pl.BlockSpec((None, rows, D_MODEL), lambda b, i, *_: (b, i, 0))


def _dot(a, b):
    return jnp.dot(a, b, preferred_element_type=F32)


def _dot_nt(a, b):
    return lax.dot_general(a, b, (((1,), (1,)), ((), ())), preferred_element_type=F32)


def _pack_words(x):
    return pltpu.bitcast(x.astype(BF16), jnp.uint32)


def _unpack_words(w):
    return pltpu.bitcast(w, BF16)


def _dot_split(a, b):
    hi = a.astype(BF16)
    lo = (a - hi.astype(F32)).astype(BF16)
    return _dot(hi, b) + _dot(lo, b)


def _ada_kernel(c_ref, w_ref, b_ref, o_ref):
    s = jax.nn.silu(c_ref[...])
    o_ref[...] = jnp.dot(s, w_ref[...], precision=lax.Precision.HIGHEST,
                         preferred_element_type=F32) + b_ref[...]


def _adaln(c_all, w_ada, b_ada):
    rows = c_all.shape[0]
    return pl.pallas_call(
        _ada_kernel,
        out_shape=jax.ShapeDtypeStruct((6, rows, D_MODEL), F32),
        grid=(6,),
        in_specs=[
            pl.BlockSpec((rows, D_MODEL), lambda j: (0, 0)),
            pl.BlockSpec((D_MODEL, D_MODEL), lambda j: (0, j)),
            pl.BlockSpec((1, D_MODEL), lambda j: (0, j)),
        ],
        out_specs=pl.BlockSpec((None, rows, D_MODEL), lambda j: (j, 0, 0)),
        compiler_params=_params("arbitrary"),
        name="adaln",
    )(c_all, w_ada, b_ada.reshape(1, 6 * D_MODEL))


def _pack_kernel(x_ref, o_ref):
    o_ref[...] = _pack_words(x_ref[...])


def _pack_t_kernel(x_ref, o_ref):
    o_ref[...] = _pack_words(x_ref[...].T)


def _pack_table(x):
    n, m = x.shape
    return pl.pallas_call(
        _pack_kernel,
        out_shape=jax.ShapeDtypeStruct((n // 2, m), jnp.uint32),
        grid=(n // PACK_ROWS,),
        in_specs=[pl.BlockSpec((PACK_ROWS, m), lambda i: (i, 0))],
        out_specs=pl.BlockSpec((PACK_ROWS // 2, m), lambda i: (i, 0)),
        compiler_params=_params("parallel"),
        name="pack_table",
    )(x)


def _pack_table_transposed(x):
    n, m = x.shape
    return pl.pallas_call(
        _pack_t_kernel,
        out_shape=jax.ShapeDtypeStruct((m // 2, n), jnp.uint32),
        grid=(n // PACK_ROWS,),
        in_specs=[pl.BlockSpec((PACK_ROWS, m), lambda i: (i, 0))],
        out_specs=pl.BlockSpec((m // 2, PACK_ROWS), lambda i: (0, i)),
        compiler_params=_params("parallel"),
        name="pack_table_t",
    )(x)


def _head_rms(z, hsum, hexp, w):
    ss = _dot((z * z).astype(BF16), hsum)
    r = lax.rsqrt(ss * (1.0 / HEAD_DIM) + RMS_EPS)
    return z * _dot_split(r, hexp) * w


def _proj_kernel(x_ref, sc_ref, sh_ref, nw_ref, win_ref, qw_ref, kw_ref, hsum_ref, hexp_ref,
                 q_ref, k2_ref, v2_ref, cu_ref, gc_ref, sa_ref):
    x = x_ref[...]
    h = x * lax.rsqrt(jnp.mean(x * x, axis=-1, keepdims=True) + RMS_EPS) * nw_ref[...]
    h = h * (1.0 + sc_ref[...]) + sh_ref[...]
    hb = h.astype(BF16)

    def seg(lo, hi):
        return _dot(hb, win_ref[:, lo:hi])

    q = _head_rms(seg(C_Q, C_K), hsum_ref[...], hexp_ref[...], qw_ref[...])
    q_ref[...] = (q * (HEAD_DIM ** -0.5)).astype(BF16)
    nk = 2 * KV_W
    k2_ref[...] = _head_rms(seg(C_K, C_V), hsum_ref[:nk, :], hexp_ref[:, :nk], kw_ref[...])
    v2_ref[...] = seg(C_V, C_U)
    cu_ref[...] = seg(C_GPRE, C_GATTN) * seg(C_U, C_GPOST)
    gc_ref[...] = (seg(C_GPOST, C_GPRE) * jax.nn.sigmoid(seg(C_GCONV, C_END))).astype(BF16)
    sa_ref[...] = jax.nn.sigmoid(seg(C_GATTN, C_GCONV)).astype(BF16)


def _project(x, sc1, sh1, nw, win2, qw, kw, hsum, hexp):
    nb, s, _ = x.shape
    rows = min(PROJ_ROWS, s)
    tok = lambda w, dt: jax.ShapeDtypeStruct((nb, s, w), dt)
    tspec = lambda w: pl.BlockSpec((None, rows, w), lambda b, i: (b, i, 0))
    return pl.pallas_call(
        _proj_kernel,
        out_shape=(tok(D_MODEL, BF16), tok(2 * KV_W, F32), tok(2 * KV_W, F32),
                   tok(D_MODEL, F32), tok(D_MODEL, BF16), tok(D_MODEL, BF16)),
        grid=(nb, s // rows),
        in_specs=[
            tspec(D_MODEL), _mod_spec(sc1, rows), _mod_spec(sh1, rows),
            _const_spec((1, D_MODEL)), _const_spec((D_MODEL, C_END)),
            _const_spec((1, D_MODEL)), _const_spec((1, 2 * KV_W)),
            _const_spec((D_MODEL, LANES)), _const_spec((LANES, D_MODEL)),
        ],
        out_specs=(tspec(D_MODEL), tspec(2 * KV_W), tspec(2 * KV_W),
                   tspec(D_MODEL), tspec(D_MODEL), tspec(D_MODEL)),
        compiler_params=_params("parallel", "parallel"),
        name="in_proj",
    )(x, sc1, sh1, nw, win2, qw, kw, hsum, hexp)


def _attend_blocks(blocks, sinks_ref):
    qb = blocks[0][0].shape[0]
    lane = lax.broadcasted_iota(jnp.int32, (qb, LANES), 1)
    low = lane < HEAD_DIM
    row = lax.broadcasted_iota(jnp.int32, (4 * qb, 2 * WINDOW), 0)
    col = lax.broadcasted_iota(jnp.int32, (4 * qb, 2 * WINDOW), 1)
    dist = col - (row & (qb - 1))
    band = (dist >= 1) & (dist <= WINDOW)
    blk = lax.broadcasted_iota(jnp.int32, (4 * qb, 1), 0) // qb
    ones = jnp.ones((2 * WINDOW, LANES), BF16)
    sinks = []
    for j in range(N_KV_HEADS):
        sink = jnp.full((4 * qb, 1), sinks_ref[4 * j + 3], F32)
        for g in range(2, -1, -1):
            sink = jnp.where(blk == g, sinks_ref[4 * j + g], sink)
        sinks.append(sink)

    scores = []
    for qblk, k2, _, _ in blocks:
        for j in range(N_KV_HEADS):
            qa = qblk[:, 2 * j * LANES:(2 * j + 1) * LANES].astype(F32)
            qc = qblk[:, (2 * j + 1) * LANES:(2 * j + 2) * LANES].astype(F32)
            lhs = jnp.concatenate([jnp.where(low, qa, 0.0), jnp.where(low, 0.0, qa),
                                   jnp.where(low, qc, 0.0), jnp.where(low, 0.0, qc)], axis=0)
            scores.append(_dot_nt(lhs.astype(BF16), k2[:, j * LANES:(j + 1) * LANES]))

    probs = []
    for b, (_, _, _, first_col) in enumerate(blocks):
        mask = band if isinstance(first_col, int) and first_col == 0 else band & (col >= first_col)
        for j in range(N_KV_HEADS):
            s = jnp.where(mask, scores[b * N_KV_HEADS + j], -jnp.inf)
            m = jnp.maximum(jnp.max(s, axis=-1, keepdims=True), sinks[j])
            probs.append((jnp.exp(s - m).astype(BF16), jnp.exp(sinks[j] - m)))

    results = []
    for b, (_, _, v2, _) in enumerate(blocks):
        outs = []
        for j in range(N_KV_HEADS):
            p, sink_term = probs[b * N_KV_HEADS + j]
            o = _dot(p, jnp.concatenate([v2[:, j * LANES:(j + 1) * LANES], ones], axis=1))
            on = o[:, :LANES] / (o[:, LANES:] + sink_term)
            outs.append(jnp.where(low, on[0:qb], on[qb:2 * qb]))
            outs.append(jnp.where(low, on[2 * qb:3 * qb], on[3 * qb:4 * qb]))
        results.append(jnp.concatenate(outs, axis=1))
    return results


def _merge_tail(x, cu, prev1, prev2, gc, sa, oattn, g1, sc2, sh2, cw_ref, cb, wout, n2w):
    conv_y = prev2 * cw_ref[0:1, :] + prev1 * cw_ref[1:2, :] + cu * cw_ref[2:3, :] + cb
    merged = sa.astype(F32) * oattn + gc.astype(F32) * conv_y
    x1 = x + g1 * _dot(merged.astype(BF16), wout)
    h2 = x1 * lax.rsqrt(jnp.mean(x1 * x1, axis=-1, keepdims=True) + RMS_EPS) * n2w
    h2 = h2 * (1.0 + sc2) + sh2
    return x1, _pack_words(h2)


def _prompt_mix_kernel(sinks_ref, x_ref, q_ref, k_ref, kp_ref, v_ref, vp_ref, cu_ref, cup_ref,
                       gc_ref, sa_ref, g1_ref, sc2_ref, sh2_ref, cw_ref, cb_ref, wout_ref, n2w_ref,
                       x1_ref, h2_ref, oat_scr):
    i = pl.program_id(1)
    rows = x_ref.shape[0]
    kext = jnp.concatenate([kp_ref[...], k_ref[...]], axis=0).astype(BF16)
    vext = jnp.concatenate([vp_ref[...], v_ref[...]], axis=0).astype(BF16)
    for w in range(rows // WINDOW):
        first_col = jnp.where(i == 0, WINDOW, 0) if w == 0 else 0
        lo = w * WINDOW
        oat_scr[lo:lo + WINDOW, :] = _attend_blocks(
            [(q_ref[lo:lo + WINDOW, :], kext[lo:lo + 2 * WINDOW, :], vext[lo:lo + 2 * WINDOW, :],
              first_col)], sinks_ref)[0]

    cu = cu_ref[...]
    has_prev = (i > 0).astype(F32)
    p1 = cup_ref[7:8, :] * has_prev
    p2 = cup_ref[6:7, :] * has_prev
    r = lax.broadcasted_iota(jnp.int32, (rows, 1), 0)
    prev1 = jnp.where(r == 0, p1, pltpu.roll(cu, 1, axis=0))
    prev2 = jnp.where(r == 0, p2, jnp.where(r == 1, p1, pltpu.roll(cu, 2, axis=0)))
    x1, h2 = _merge_tail(x_ref[...], cu, prev1, prev2, gc_ref[...], sa_ref[...], oat_scr[...],
                         g1_ref[...], sc2_ref[...], sh2_ref[...], cw_ref, cb_ref[...],
                         wout_ref[...], n2w_ref[...])
    x1_ref[...] = x1
    h2_ref[...] = h2


def _prompt_mix(sinks, x, q, k2, v2, cu, gc, sa, g1, sc2, sh2, cw, cb, wout, n2w):
    nb, s, _ = x.shape
    rows = ATTN_ROWS
    wpt = rows // WINDOW
    tspec = lambda w: pl.BlockSpec((None, rows, w), lambda b, i: (b, i, 0))
    prev_win = pl.BlockSpec((None, WINDOW, 2 * KV_W), lambda b, i: (b, jnp.maximum(i * wpt - 1, 0), 0))
    prev_rows = pl.BlockSpec((None, 8, D_MODEL), lambda b, i: (b, jnp.maximum(i * (rows // 8) - 1, 0), 0))
    mod = pl.BlockSpec((None, 1, D_MODEL), lambda b, i: (b, 0, 0))
    return pl.pallas_call(
        _prompt_mix_kernel,
        out_shape=(jax.ShapeDtypeStruct((nb, s, D_MODEL), F32),
                   jax.ShapeDtypeStruct((nb, s // 2, D_MODEL), jnp.uint32)),
        grid=(nb, s // rows),
        in_specs=[
            pl.BlockSpec(memory_space=pltpu.SMEM),
            tspec(D_MODEL), tspec(D_MODEL),
            tspec(2 * KV_W), prev_win, tspec(2 * KV_W), prev_win,
            tspec(D_MODEL), prev_rows, tspec(D_MODEL), tspec(D_MODEL),
            mod, mod, mod,
            _const_spec((CONV_W, D_MODEL)), _const_spec((1, D_MODEL)),
            _const_spec((D_MODEL, D_MODEL)), _const_spec((1, D_MODEL)),
        ],
        out_specs=(tspec(D_MODEL), pl.BlockSpec((None, rows // 2, D_MODEL), lambda b, i: (b, i, 0))),
        scratch_shapes=[pltpu.VMEM((rows, D_MODEL), F32)],
        compiler_params=_params("parallel", "parallel"),
        name="prompt_mix",
    )(sinks, x, q, k2, k2, v2, v2, cu, cu, gc, sa, g1, sc2, sh2, cw, cb, wout, n2w)


def _dup_heads_lanes(x):
    low = lax.broadcasted_iota(jnp.int32, (x.shape[0], LANES), 1) < HEAD_DIM
    out = []
    for p in range(x.shape[1] // LANES):
        pair = x[:, p * LANES:(p + 1) * LANES]
        swapped = pltpu.roll(pair, HEAD_DIM, axis=1)
        out += [jnp.where(low, pair, swapped), jnp.where(low, swapped, pair)]
    return jnp.concatenate(out, axis=1).astype(BF16)


def _sample_attn_kernel(sinks_ref, q_ref, ck_ref, kn_ref, cv_ref, vn_ref, o_ref):
    pad = jnp.zeros((WINDOW - SAMPLE_Q_ROWS, 2 * KV_W), BF16)
    blocks = []
    for n in range(q_ref.shape[0]):
        kext = jnp.concatenate([_dup_heads_lanes(ck_ref[n]), kn_ref[n], pad], axis=0)
        vext = jnp.concatenate([_dup_heads_lanes(cv_ref[n]), vn_ref[n], pad], axis=0)
        blocks.append((q_ref[n], kext, vext, 0))
    for n, o in enumerate(_attend_blocks(blocks, sinks_ref)):
        o_ref[n] = o


def _sample_attn(sinks, q8, ck2, k2n, cv2, v2n):
    nseq = q8.shape[0]
    sb = SAMPLE_SEQ_BLOCK
    spec = lambda r, w: pl.BlockSpec((sb, r, w), lambda n: (n, 0, 0))
    return pl.pallas_call(
        _sample_attn_kernel,
        out_shape=jax.ShapeDtypeStruct((nseq, SAMPLE_Q_ROWS, D_MODEL), F32),
        grid=(nseq // sb,),
        in_specs=[
            pl.BlockSpec(memory_space=pltpu.SMEM),
            spec(SAMPLE_Q_ROWS, D_MODEL),
            spec(WINDOW, KV_W), spec(SAMPLE_Q_ROWS, 2 * KV_W),
            spec(WINDOW, KV_W), spec(SAMPLE_Q_ROWS, 2 * KV_W),
        ],
        out_specs=spec(SAMPLE_Q_ROWS, D_MODEL),
        compiler_params=_params("parallel"),
        name="sample_attn",
    )(sinks, q8, ck2, k2n, cv2, v2n)


def _sample_mix_kernel(x_ref, cu_ref, p1_ref, p2_ref, gc_ref, sa_ref, oat_ref,
                       g1_ref, sc2_ref, sh2_ref, cw_ref, cb_ref, wout_ref, n2w_ref,
                       x1_ref, h2_ref):
    x1, h2 = _merge_tail(x_ref[...], cu_ref[...], p1_ref[...], p2_ref[...], gc_ref[...],
                         sa_ref[...], oat_ref[...], g1_ref[...], sc2_ref[...], sh2_ref[...],
                         cw_ref, cb_ref[...], wout_ref[...], n2w_ref[...])
    x1_ref[...] = x1
    h2_ref[...] = h2


def _sample_mix(x, cu, prev1, prev2, gc, sa, oattn, g1, sc2, sh2, cw, cb, wout, n2w):
    t = x.shape[0]
    full = lambda: pl.BlockSpec((t, D_MODEL), lambda: (0, 0))
    return pl.pallas_call(
        _sample_mix_kernel,
        out_shape=(jax.ShapeDtypeStruct((t, D_MODEL), F32),
                   jax.ShapeDtypeStruct((t // 2, D_MODEL), jnp.uint32)),
        in_specs=[full() for _ in range(10)] + [
            pl.BlockSpec((CONV_W, D_MODEL), lambda: (0, 0)),
            pl.BlockSpec((1, D_MODEL), lambda: (0, 0)),
            pl.BlockSpec((D_MODEL, D_MODEL), lambda: (0, 0)),
            pl.BlockSpec((1, D_MODEL), lambda: (0, 0)),
        ],
        out_specs=(full(), pl.BlockSpec((t // 2, D_MODEL), lambda: (0, 0))),
        compiler_params=pltpu.CompilerParams(vmem_limit_bytes=VMEM_LIMIT),
        name="sample_mix",
    )(x, cu, prev1, prev2, gc, sa, oattn, g1, sc2, sh2, cw, cb, wout, n2w)


def _top16(s, index):
    rank = jnp.full(s.shape, float(PEER_TOPK), F32)
    vals = []
    for r in range(PEER_TOPK):
        m = jnp.max(s, axis=0, keepdims=True)
        first = jnp.min(jnp.where(s == m, index, float(s.shape[0])), axis=0, keepdims=True)
        hit = index == first
        rank = jnp.where(hit, float(r), rank)
        s = jnp.where(hit, -jnp.inf, s)
        vals.append(m)
    return vals, rank


def _row_array(rows, index):
    out = jnp.zeros(index.shape, F32)
    for r, v in enumerate(rows):
        out = jnp.where(index == float(r), v, out)
    return out


def _gates(s1, s2, v1_0, v2_0, z):
    return jnp.exp(s1 - v1_0) * (0.5 / z), jnp.exp(s2 - v2_0)


def _route_exact(s1, s2, idx16, idx128, idx256):
    v1, rank1 = _top16(s1, idx128)
    v2, rank2 = _top16(s2, idx128)
    v2a = _row_array(v2, idx16)
    cand = jnp.concatenate([v1[r] + v2a for r in range(PEER_TOPK)], axis=0)
    sel = jnp.zeros(cand.shape, F32)
    c = cand
    for _ in range(PEER_TOPK):
        m = jnp.max(c, axis=0, keepdims=True)
        first = jnp.min(jnp.where(c == m, idx256, float(PEER_TOPK * PEER_TOPK)),
                        axis=0, keepdims=True)
        hit = idx256 == first
        sel = jnp.where(hit, 1.0, sel)
        c = jnp.where(hit, -jnp.inf, c)
    z = jnp.sum(sel * jnp.exp(cand - (v1[0] + v2[0])), axis=0, keepdims=True)
    n1 = jnp.zeros(rank1.shape, F32)
    for r in range(PEER_TOPK):
        n_r = jnp.sum(sel[r * PEER_TOPK:(r + 1) * PEER_TOPK], axis=0, keepdims=True)
        n1 = jnp.where(rank1 == float(r), n_r, n1)
    e1, e2 = _gates(s1, s2, v1[0], v2[0], z)
    return n1, e1, rank2, e2


SUBLANES = 8


def _batcher_pairs(n):
    pairs = []
    p = 1
    while p < n:
        k = p
        while k >= 1:
            for j in range(k % p, n - k, 2 * k):
                for i in range(min(k, n - j - k)):
                    if (i + j) // (2 * p) == (i + j + k) // (2 * p):
                        pairs.append((i + j, i + j + k))
            k //= 2
        p *= 2
    return pairs


def _sort_desc(vs):
    vs = list(vs)
    for i, j in _batcher_pairs(len(vs)):
        vs[i], vs[j] = jnp.maximum(vs[i], vs[j]), jnp.minimum(vs[i], vs[j])
    return vs


def _merge_top(a, b):
    n = len(a)
    c = [jnp.maximum(a[i], b[n - 1 - i]) for i in range(n)]
    d = n // 2
    while d >= 1:
        for i in range(n):
            if i & d == 0:
                c[i], c[i + d] = jnp.maximum(c[i], c[i + d]), jnp.minimum(c[i], c[i + d])
        d //= 2
    return c


def _merge_sublanes(w):
    for shift in (4, 2, 1):
        w = _merge_top(w, [pltpu.roll(x, shift, axis=0) for x in w])
    return w


def _tiles(s):
    return [s[SUBLANES * j:SUBLANES * (j + 1)] for j in range(s.shape[0] // SUBLANES)]


def _has_adjacent_tie(v):
    gap = v[0] - v[1]
    for r in range(1, len(v) - 1):
        gap = jnp.minimum(gap, v[r] - v[r + 1])
    return gap == 0.0


def _count_rows(flags):
    acc = flags[0]
    for f in flags[1:]:
        acc = acc + f
    return jnp.sum(acc, axis=0, keepdims=True)


def _route_distinct(s1, s2, sub):
    k = PEER_TOPK
    t1, t2 = _tiles(s1), _tiles(s2)
    v1 = _merge_sublanes(_sort_desc(t1))
    v2 = _merge_sublanes(_sort_desc(t2))

    pick = lambda v: jnp.where(sub < 2.0, v[0], jnp.where(sub < 4.0, v[1], jnp.where(sub < 6.0, v[2], v[3])))
    a_const, b_const = pick(v1), pick(v2)
    lengths = (16, 15, 7, 6, 3, 2, 1)
    lists = []
    for i in range(k):
        a, b = a_const, b_const
        for s_col, s_row, off in ((5.0, 4.0, 2), (3.0, 2.0, 1), (1.0, 0.0, 0)):
            if i < lengths[int(s_col)]:
                a = jnp.where(sub == s_col, v1[i + off + 1], a)
            if i < lengths[int(s_row)]:
                b = jnp.where(sub == s_row, v2[i + off], b)
        live = float(sum(1 for n in lengths if n > i))
        lists.append(jnp.where(sub < live, a + b, -jnp.inf))
    c = _merge_sublanes(lists)
    tau = c[k - 1]

    z = None
    for x in lists:
        e = jnp.where(x >= tau, jnp.exp(x - c[0]), 0.0)
        z = e if z is None else z + e
    z = jnp.sum(z, axis=0, keepdims=True)

    counts, total = [], None
    for r1 in range(k):
        n_r = jnp.zeros_like(tau)
        for r2 in range(k // (r1 + 1)):
            n_r = jnp.where(v1[r1] + v2[r2] >= tau, float(r2 + 1), n_r)
        counts.append(n_r)
        total = n_r if total is None else total + n_r

    n1, rank2, lead1, lead2 = [], [], [], []
    for x1, x2 in zip(t1, t2):
        n = jnp.zeros_like(x1)
        for r in range(k - 1, -1, -1):
            n = jnp.where(x1 >= v1[r], counts[r], n)
        rk = jnp.zeros_like(x2)
        for r in range(k):
            rk = jnp.where(v2[r] > x2, float(r + 1), rk)
        n1.append(n)
        rank2.append(rk)
        lead1.append(jnp.where(x1 >= v1[k - 1], 1.0, 0.0))
        lead2.append(jnp.where(x2 >= v2[k - 1], 1.0, 0.0))
    n1 = jnp.concatenate(n1, axis=0)
    rank2 = jnp.concatenate(rank2, axis=0)

    want = float(k)
    tie = _has_adjacent_tie(v1) | _has_adjacent_tie(v2) | _has_adjacent_tie(c)
    valid = (~tie) & (_count_rows(lead1) == want) & (_count_rows(lead2) == want) & (total == want)
    e1, e2 = _gates(s1, s2, v1[0][0:1], v2[0][0:1], z)
    return n1, e1, rank2, e2, valid


def _route_kernel(h2_ref, wqt_ref, keys_ref, n1_ref, e1_ref, r2_ref, e2_ref, qpt_scr):
    rows = 2 * h2_ref.shape[0]
    qpt_scr[...] = _dot_nt(wqt_ref[...], _unpack_words(h2_ref[...]))
    idx8 = lax.broadcasted_iota(jnp.int32, (8, LANES), 0).astype(F32)
    idx16 = lax.broadcasted_iota(jnp.int32, (PEER_TOPK, LANES), 0).astype(F32)

    groups = PEER_HEADS // ROUTE_HEADS

    def body(it, carry):
        tc = it // groups
        ls = pl.ds(pl.multiple_of(tc * LANES, LANES), LANES)
        hp = lax.Precision.HIGHEST
        heads, scores = [], []
        for j in range(ROUTE_HEADS):
            h = (it % groups) * ROUTE_HEADS + j
            base = pl.multiple_of(h * (2 * PEER_HALF), 2 * PEER_HALF)
            q1 = qpt_scr[pl.ds(base, PEER_HALF), ls]
            q2 = qpt_scr[pl.ds(base + PEER_HALF, PEER_HALF), ls]
            s1 = jnp.dot(keys_ref[2 * h], q1, precision=hp, preferred_element_type=F32)
            s2 = jnp.dot(keys_ref[2 * h + 1], q2, precision=hp, preferred_element_type=F32)
            heads.append(h)
            scores.append((s1, s2))

        def store(h, n1, e1, rank2, e2):
            n1_ref[h, :, ls] = n1
            e1_ref[h, :, ls] = e1
            r2_ref[h, :, ls] = _pack_words(rank2)
            e2_ref[h, :, ls] = _pack_words(e2)

        tied = []
        for h, (s1, s2) in zip(heads, scores):
            n1, e1, rank2, e2, valid = _route_distinct(s1, s2, idx8)
            store(h, n1, e1, rank2, e2)
            tied.append(jnp.max(jnp.where(valid, 0.0, 1.0)) > 0.0)

        for h, (s1, s2), redo in zip(heads, scores, tied):
            @pl.when(redo)
            def _():
                idx128 = lax.broadcasted_iota(jnp.int32, (PEER_KEYS, LANES), 0).astype(F32)
                idx256 = lax.broadcasted_iota(jnp.int32, (PEER_TOPK * PEER_TOPK, LANES), 0).astype(F32)
                store(h, *_route_exact(s1, s2, idx16, idx128, idx256))

        return carry

    lax.fori_loop(0, (rows // LANES) * groups, body, 0)


def _route(h2, wqt, keys):
    t = 2 * h2.shape[0]
    rows = min(ROUTE_ROWS, t)
    out = jax.ShapeDtypeStruct((PEER_HEADS, PEER_KEYS, t), F32)
    outw = jax.ShapeDtypeStruct((PEER_HEADS, PEER_KEYS // 2, t), jnp.uint32)
    ospec = pl.BlockSpec((PEER_HEADS, PEER_KEYS, rows), lambda i: (0, 0, i))
    wspec = pl.BlockSpec((PEER_HEADS, PEER_KEYS // 2, rows), lambda i: (0, 0, i))
    return pl.pallas_call(
        _route_kernel,
        out_shape=(out, out, outw, outw),
        grid=(t // rows,),
        in_specs=[
            pl.BlockSpec((rows // 2, D_MODEL), lambda i: (i, 0)),
            _const_spec((D_MODEL, D_MODEL)),
            _const_spec((2 * PEER_HEADS, PEER_KEYS, PEER_HALF)),
        ],
        out_specs=(ospec, ospec, wspec, wspec),
        scratch_shapes=[pltpu.VMEM((D_MODEL, rows), F32)],
        compiler_params=_params("parallel"),
        name="peer_route",
    )(h2, wqt, keys)


PACK = 16


def _gelu_doubled(a):
    return a * (1.0 + lax.erf(a * (2.0 ** -0.5)))


def _expert_weights(n1_ref, e1_ref, r2_ref, e2_ref, at_ref, hw_ref, tc, il):
    ls = slice(tc * LANES, (tc + 1) * LANES)
    gates = [None] * (PEER_KEYS // PACK)
    for h in range(PEER_HEADS):
        nb = jnp.broadcast_to(n1_ref[h, il:il + 1, ls], (PACK, LANES)).astype(BF16)
        eb = jnp.broadcast_to(e1_ref[h, il:il + 1, ls], (PACK, LANES)).astype(BF16)
        for k in range(PEER_KEYS // PACK):
            ws = slice(PACK // 2 * k, PACK // 2 * (k + 1))
            e2 = _unpack_words(e2_ref[h, ws, ls])
            keep = _unpack_words(r2_ref[h, ws, ls]) < nb
            if h == 0:
                gates[k] = jnp.where(keep, e2 * eb, jnp.zeros_like(e2))
            else:
                gates[k] = jnp.where(keep, gates[k] + e2 * eb, gates[k])
    for k in range(PEER_KEYS // PACK):
        e0 = il * PEER_KEYS + PACK * k
        act = _gelu_doubled(at_ref[e0:e0 + PACK, ls]).astype(BF16)
        hw_ref[e0 // 2:(e0 + PACK) // 2, ls] = pltpu.bitcast(gates[k] * act, jnp.uint32)


MXU_COLS = 256


def _peer_kernel(h2_ref, u_ref, vt_ref,
                 n1p_ref, e1p_ref, r2p_ref, e2p_ref, n1c_ref, e1c_ref, r2c_ref, e2c_ref,
                 x1_ref, g2_ref, y_ref, at0_scr, at1_scr, hw0_scr, hw1_scr, acc_scr,
                 *, pairs_per_block):
    g = pl.program_id(0)
    ec = PEER_CHUNK
    rows = x1_ref.shape[0]

    @pl.when(g == 0)
    def _():
        at1_scr[...] = jnp.zeros_like(at1_scr)
        hw0_scr[...] = jnp.zeros_like(hw0_scr)
        hw1_scr[...] = jnp.zeros_like(hw1_scr)
        acc_scr[...] = jnp.zeros_like(acc_scr)

    pair = jnp.maximum(g - 1, 0) % pairs_per_block

    def half_step(uw_row0, routing, at_w, at_r, hw_w, hw_r, vtw_col0, restart):
        for n in range(rows // MXU_COLS):
            ts = slice(n * MXU_COLS, (n + 1) * MXU_COLS)
            tw = slice(n * MXU_COLS // 2, (n + 1) * MXU_COLS // 2)
            at_w[:, ts] = _dot_nt(_unpack_words(u_ref[uw_row0:uw_row0 + ec // 2, :]),
                                  _unpack_words(h2_ref[tw, :]))
            for tc in range(n * (MXU_COLS // LANES), (n + 1) * (MXU_COLS // LANES)):
                for il in range(ec // PEER_KEYS):
                    _expert_weights(*routing, at_r, hw_w, tc, il)
            acc = acc_scr[:, ts]
            if restart is not None:
                acc = jnp.where(restart, 0.0, acc)
            acc_scr[:, ts] = acc + _dot(_unpack_words(vt_ref[:, vtw_col0:vtw_col0 + ec]),
                                        _unpack_words(hw_r[:, ts]))

    half_step(0, (n1p_ref, e1p_ref, r2p_ref, e2p_ref), at0_scr, at1_scr,
              hw1_scr, hw0_scr, 0, pair == 0)
    half_step(ec // 2, (n1c_ref, e1c_ref, r2c_ref, e2c_ref), at1_scr, at0_scr,
              hw0_scr, hw1_scr, ec, None)

    @pl.when((g > 0) & (pair == pairs_per_block - 1))
    def _():
        y_ref[...] = x1_ref[...] + g2_ref[...] * acc_scr[...].T


def _peer(h2, u, vt, n1, e1, r2, e2, x1, g2):
    t = x1.shape[0]
    rows = min(PEER_ROWS, t)
    n_chunks = PEER_EXPERTS // PEER_CHUNK
    ppb = n_chunks // 2
    n_pairs = (t // rows) * ppb
    n_items = 2 * n_pairs
    il_rows = PEER_CHUNK // PEER_KEYS

    pair_a = lambda g: jnp.minimum(g, n_pairs - 1)
    item_p = lambda g: jnp.maximum(2 * g - 1, 0)
    item_c = lambda g: jnp.minimum(2 * g, n_items - 1)
    block_c = lambda g: jnp.maximum(g - 1, 0) // ppb

    def routing_specs(item):
        per_chunk = pl.BlockSpec((PEER_HEADS, il_rows, rows),
                                 lambda g: (0, item(g) % n_chunks, item(g) // n_chunks))
        per_block = pl.BlockSpec((PEER_HEADS, PEER_KEYS // 2, rows),
                                 lambda g: (0, 0, item(g) // n_chunks))
        return [per_chunk, per_chunk, per_block, per_block]

    if g2.shape[1] == 1:
        blocks_per_seq = (t // g2.shape[0]) // rows
        g2_spec = pl.BlockSpec((None, 1, D_MODEL), lambda g: (block_c(g) // blocks_per_seq, 0, 0))
    else:
        g2_spec = pl.BlockSpec((None, rows, D_MODEL), lambda g: (0, block_c(g), 0))
    tok_c = pl.BlockSpec((rows, D_MODEL), lambda g: (block_c(g), 0))
    return pl.pallas_call(
        functools.partial(_peer_kernel, pairs_per_block=ppb),
        out_shape=jax.ShapeDtypeStruct((t, D_MODEL), F32),
        grid=(n_pairs + 1,),
        in_specs=[
            pl.BlockSpec((rows // 2, D_MODEL), lambda g: (pair_a(g) // ppb, 0)),
            pl.BlockSpec((PEER_CHUNK, D_MODEL), lambda g: (pair_a(g) % ppb, 0)),
            pl.BlockSpec((D_MODEL // 2, 2 * PEER_CHUNK), lambda g: (0, jnp.maximum(g - 1, 0) % ppb)),
            *routing_specs(item_p), *routing_specs(item_c),
            tok_c, g2_spec,
        ],
        out_specs=tok_c,
        scratch_shapes=[pltpu.VMEM((PEER_CHUNK, rows), F32),
                        pltpu.VMEM((PEER_CHUNK, rows), F32),
                        pltpu.VMEM((PEER_CHUNK // 2, rows), jnp.uint32),
                        pltpu.VMEM((PEER_CHUNK // 2, rows), jnp.uint32),
                        pltpu.VMEM((D_MODEL, rows), F32)],
        compiler_params=_params("arbitrary"),
        name="peer_experts",
    )(h2, u, vt, n1, e1, r2, e2, n1, e1, r2, e2, x1, g2)


def _dup_heads(w):
    lead = w.shape[:-1]
    w = w.reshape(lead + (N_KV_HEADS, 1, HEAD_DIM))
    return jnp.broadcast_to(w, lead + (N_KV_HEADS, 2, HEAD_DIM)).reshape(lead + (2 * KV_W,))


def _first_of_dup(a):
    return a.reshape(a.shape[:-1] + (N_KV_HEADS, 2, HEAD_DIM))[..., 0, :]


def kernel(x_prompt, x_sample, cache_k, cache_v, state_conv, c_prompt, c_sample, w_ada, b_ada, norm1_w, w_in, q_norm_w, k_norm_w, sinks, conv_w, conv_bias, w_out, norm2_w, w_query, sub_keys, expert_u, expert_v):
    depth = w_ada.shape[0]
    batch, seq, _ = x_prompt.shape
    nseq, dec = x_sample.shape[:2]
    assert dec <= SAMPLE_Q_ROWS and dec >= CONV_W - 1
    ts = nseq * dec

    head_of_col = jnp.arange(D_MODEL, dtype=jnp.int32) // HEAD_DIM
    hsum = (head_of_col[:, None] == jnp.arange(LANES, dtype=jnp.int32)[None, :]).astype(BF16)
    hexp = hsum.T
    c_all = jnp.concatenate([c_prompt, c_sample], axis=0)
    c_all = jnp.pad(c_all, ((0, (-c_all.shape[0]) % 8), (0, 0)))

    yp = x_prompt
    ys = x_sample.reshape(1, ts, D_MODEL)
    kp, vp, cp, ksm, vsm, csm = [], [], [], [], [], []
    for l in range(depth):
        win = w_in[l]
        win2 = jnp.concatenate(
            [win[:, :D_MODEL], _dup_heads(win[:, D_MODEL:D_MODEL + KV_W]),
             _dup_heads(win[:, D_MODEL + KV_W:D_MODEL + 2 * KV_W]), win[:, D_MODEL + 2 * KV_W:]],
            axis=1).astype(BF16)
        nw1 = norm1_w[l].reshape(1, D_MODEL)
        nw2 = norm2_w[l].reshape(1, D_MODEL)
        qw = jnp.tile(q_norm_w[l], N_HEADS).reshape(1, D_MODEL)
        kw = jnp.tile(k_norm_w[l], 2 * N_KV_HEADS).reshape(1, 2 * KV_W)
        cw = conv_w[l]
        cb = conv_bias[l].reshape(1, D_MODEL)
        wout = w_out[l].astype(BF16)
        wqt = w_query[l].T.astype(BF16)
        keys = sub_keys[l].reshape(2 * PEER_HEADS, PEER_KEYS, PEER_HALF)
        u = _pack_table(expert_u[l])
        vt = _pack_table_transposed(expert_v[l])

        mod = _adaln(c_all, w_ada[l], b_ada[l])
        mod_p = [mod[j, :batch].reshape(batch, 1, D_MODEL) for j in range(6)]
        mod_s = [jnp.repeat(mod[j, batch:batch + nseq], dec, axis=0).reshape(1, ts, D_MODEL)
                 for j in range(6)]

        q, k2, v2, cu, gc, sa = _project(yp, mod_p[1], mod_p[0], nw1, win2, qw, kw, hsum, hexp)
        x1, h2 = _prompt_mix(sinks[l], yp, q, k2, v2, cu, gc, sa, mod_p[2], mod_p[4], mod_p[3],
                             cw, cb, wout, nw2)
        h2 = h2.reshape(batch * seq // 2, D_MODEL)
        n1, e1, r2, e2 = _route(h2, wqt, keys)
        yp = _peer(h2, u, vt, n1, e1, r2, e2, x1.reshape(batch * seq, D_MODEL), mod_p[5])
        yp = yp.reshape(batch, seq, D_MODEL)
        kp.append(_first_of_dup(k2[:, seq - WINDOW:]))
        vp.append(_first_of_dup(v2[:, seq - WINDOW:]))
        cp.append(cu[:, seq - (CONV_W - 1):])

        q, k2, v2, cu, gc, sa = _project(ys, mod_s[1], mod_s[0], nw1, win2, qw, kw, hsum, hexp)
        pad_rows = lambda a: jnp.pad(a.reshape(nseq, dec, a.shape[-1]),
                                     ((0, 0), (0, SAMPLE_Q_ROWS - dec), (0, 0)))
        oat = _sample_attn(sinks[l], pad_rows(q[0]),
                           cache_k[l].reshape(nseq, WINDOW, KV_W), pad_rows(k2[0]).astype(BF16),
                           cache_v[l].reshape(nseq, WINDOW, KV_W), pad_rows(v2[0]).astype(BF16))
        oat = oat[:, :dec].reshape(ts, D_MODEL)
        cu_seq = cu.reshape(nseq, dec, D_MODEL)
        up = jnp.concatenate([state_conv[l], cu_seq], axis=1)
        prev2 = up[:, 0:dec].reshape(ts, D_MODEL)
        prev1 = up[:, 1:dec + 1].reshape(ts, D_MODEL)
        flat = lambda a: a.reshape(ts, D_MODEL)
        x1, h2 = _sample_mix(flat(ys), flat(cu), prev1, prev2, flat(gc), flat(sa), oat,
                             flat(mod_s[2]), flat(mod_s[4]), flat(mod_s[3]), cw, cb, wout, nw2)
        n1, e1, r2, e2 = _route(h2, wqt, keys)
        ys = _peer(h2, u, vt, n1, e1, r2, e2, x1, mod_s[5]).reshape(1, ts, D_MODEL)
        k_new = _first_of_dup(k2[0]).reshape(nseq, dec, N_KV_HEADS, HEAD_DIM)
        v_new = _first_of_dup(v2[0]).reshape(nseq, dec, N_KV_HEADS, HEAD_DIM)
        ksm.append(jnp.concatenate([cache_k[l], k_new], axis=1)[:, -WINDOW:])
        vsm.append(jnp.concatenate([cache_v[l], v_new], axis=1)[:, -WINDOW:])
        csm.append(up[:, -(CONV_W - 1):])

    return (yp, ys.reshape(nseq, dec, D_MODEL), jnp.stack(kp), jnp.stack(vp), jnp.stack(cp),
            jnp.stack(ksm), jnp.stack(vsm), jnp.stack(csm))
```
